```python
import jax, jax.numpy as jnp
from jax import lax
import numpy as np

D_MODEL = 1024
BATCH = 16
SEQ = 2048
DEPTH = 4

BRANCH_W = D_MODEL // 2
MLSTM_HEADS = 4
MLSTM_DV = BRANCH_W // MLSTM_HEADS
MLSTM_DQK = MLSTM_DV // 2
MLSTM_CONV_K = 4
HGRN_HEADS = 4
HGRN_DK = BRANCH_W // HGRN_HEADS
HGRN_DV = BRANCH_W // HGRN_HEADS
CONV_CH = BRANCH_W
CONV_K = 31
N_BRANCH = 3
D_FF = -(-8 * D_MODEL // (3 * 256)) * 256
CHUNK = 64
EPS = 1e-6
NEG_BIG = -1e30
F_FLOOR = 1e-30
MLSTM_F_BIAS_LO = 3.0
MLSTM_F_BIAS_HI = 6.0

COL_SIZES = (
    2 * MLSTM_HEADS * MLSTM_DQK,
    MLSTM_HEADS * MLSTM_DV,
    MLSTM_HEADS * MLSTM_DV,
    MLSTM_HEADS,
    MLSTM_HEADS,
    HGRN_HEADS * HGRN_DK,
    HGRN_HEADS * HGRN_DK,
    HGRN_HEADS * HGRN_DV,
    HGRN_HEADS * HGRN_DV,
    CONV_CH,
    CONV_CH,
    N_BRANCH * D_MODEL,
)
D_IN = sum(COL_SIZES)

kernel_name = "hybrid_mlstm_hgrn2_conformer_griffin_merge"


def _col_offsets():
    offs = [0]
    for s in COL_SIZES:
        offs.append(offs[-1] + s)
    return offs


def rmsnorm(x, g):
    xf = x.astype(jnp.float32)
    y = xf * lax.rsqrt(jnp.mean(xf * xf, axis=-1, keepdims=True) + EPS)
    return (y * g.astype(jnp.float32)).astype(x.dtype)


def layernorm(x, g, b):
    xf = x.astype(jnp.float32)
    mu = jnp.mean(xf, axis=-1, keepdims=True)
    var = jnp.mean(jnp.square(xf - mu), axis=-1, keepdims=True)
    y = (xf - mu) * lax.rsqrt(var + EPS)
    return (y * g.astype(jnp.float32) + b.astype(jnp.float32)).astype(x.dtype)


def causal_depthwise_conv(x, w, b):
    k = w.shape[0]
    out = lax.conv_general_dilated(
        x, w.astype(x.dtype)[:, None, :], window_strides=(1,), padding=((k - 1, 0),),
        dimension_numbers=("NWC", "WIO", "NWC"), feature_group_count=x.shape[-1])
    return out + b.astype(x.dtype)


def _to_chunks(t):
    bsz, s, h, d = t.shape
    return t.reshape(bsz, s // CHUNK, CHUNK, h, d).transpose(1, 0, 3, 2, 4)


def _from_chunks(t):
    n, bsz, h, l, d = t.shape
    return t.transpose(1, 0, 3, 2, 4).reshape(bsz, n * l, h, d)


def mlstm_chunkwise(q, k, v, log_i, log_f):
    bsz, s, h, dk = q.shape
    dv = v.shape[-1]
    f32 = jnp.float32
    qc = _to_chunks(q.astype(f32) * (dk ** -0.5))
    kc = _to_chunks(k.astype(f32))
    vc = _to_chunks(v.astype(f32))
    ic = _to_chunks(log_i[..., None])[..., 0]
    fc = _to_chunks(log_f[..., None])[..., 0]
    causal = jnp.tril(jnp.ones((CHUNK, CHUNK), dtype=bool))

    def step(carry, inp):
        c_st, n_st, m_st = carry
        q_, k_, v_, li, lf = inp
        b = jnp.cumsum(lf, axis=-1)
        dlog = jnp.where(causal, b[..., :, None] - b[..., None, :] + li[..., None, :], NEG_BIG)
        g = b + m_st[..., None]
        m_t = jnp.maximum(g, jnp.max(dlog, axis=-1))
        w = jnp.where(causal, jnp.exp(dlog - m_t[..., None]), 0.0)
        inter = jnp.exp(g - m_t)
        sc = jnp.einsum("bhtd,bhsd->bhts", q_, k_) * w
        num = jnp.einsum("bhts,bhsv->bhtv", sc, v_) + inter[..., None] * jnp.einsum("bhtd,bhvd->bhtv", q_, c_st)
        den = jnp.sum(sc, axis=-1) + inter * jnp.einsum("bhtd,bhd->bht", q_, n_st)
        h_out = num / jnp.maximum(jnp.abs(den), jnp.exp(-m_t))[..., None]
        b_last = b[..., -1]
        dec = b_last[..., None] - b + li
        m_new = jnp.maximum(b_last + m_st, jnp.max(dec, axis=-1))
        ws = jnp.exp(dec - m_new[..., None])
        keep = jnp.exp(b_last + m_st - m_new)
        c_new = keep[..., None, None] * c_st + jnp.einsum("bhs,bhsv,bhsd->bhvd", ws, v_, k_)
        n_new = keep[..., None] * n_st + jnp.einsum("bhs,bhsd->bhd", ws, k_)
        return (c_new, n_new, m_new), h_out

    init = (jnp.zeros((bsz, h, dv, dk), f32), jnp.zeros((bsz, h, dk), f32), jnp.zeros((bsz, h), f32))
    _, hs = lax.scan(step, init, (qc, kc, vc, ic, fc))
    return _from_chunks(hs)


def hgrn2_chunkwise(q, k, v, log_f):
    bsz, s, h, dk = q.shape
    dv = v.shape[-1]
    qc = _to_chunks(q * (dk ** -0.5))
    kc = _to_chunks(k)
    vc = _to_chunks(v)
    fc = _to_chunks(log_f)
    causal = jnp.tril(jnp.ones((CHUNK, CHUNK), dtype=bool))[:, :, None]

    def step(s_st, inp):
        q_, k_, v_, lf = inp
        b = jnp.cumsum(lf, axis=2)
        diff = b[:, :, :, None, :] - b[:, :, None, :, :]
        decay = jnp.where(causal, jnp.exp(jnp.where(causal, diff, 0.0)), 0.0)
        a = jnp.einsum("bhtc,bhsc,bhtsc->bhts", q_, k_, decay)
        o = jnp.einsum("bhts,bhsv->bhtv", a, v_) + jnp.einsum("bhtc,bhcv->bhtv", q_ * jnp.exp(b), s_st)
        b_last = b[:, :, -1]
        s_new = jnp.exp(b_last)[..., None] * s_st + jnp.einsum(
            "bhsc,bhsv->bhcv", k_ * jnp.exp(b_last[:, :, None] - b), v_)
        return s_new, o

    _, os_ = lax.scan(step, jnp.zeros((bsz, h, dk, dv), jnp.float32), (qc, kc, vc, fc))
    return _from_chunks(os_)


def mlstm_branch(m_qk, m_v, m_o, m_i, m_f, conv_w, conv_b, norm_g):
    bsz, s, _ = m_qk.shape
    f32 = jnp.float32
    qk = jax.nn.silu(causal_depthwise_conv(m_qk, conv_w, conv_b))
    q, k = jnp.split(qk, 2, axis=-1)
    q = q.reshape(bsz, s, MLSTM_HEADS, MLSTM_DQK)
    k = k.reshape(bsz, s, MLSTM_HEADS, MLSTM_DQK)
    v = m_v.reshape(bsz, s, MLSTM_HEADS, MLSTM_DV)
    h = mlstm_chunkwise(q, k, v, m_i.astype(f32), jax.nn.log_sigmoid(m_f.astype(f32)))
    hn = rmsnorm(h, norm_g.reshape(MLSTM_HEADS, MLSTM_DV)).reshape(bsz, s, -1)
    return (jax.nn.sigmoid(m_o.astype(f32)) * hn).astype(m_o.dtype)


def hgrn2_branch(h_f, h_q, h_i, h_g, lower_bound, norm_g):
    bsz, s, _ = h_f.shape
    f32 = jnp.float32
    zf = h_f.astype(f32).reshape(bsz, s, HGRN_HEADS, HGRN_DK)
    lb = lower_bound.reshape(HGRN_HEADS, HGRN_DK)
    f = lb + (1.0 - lb) * jax.nn.sigmoid(zf)
    log_f = jnp.log(jnp.maximum(f, F_FLOOR))
    k = (1.0 - lb) * jax.nn.sigmoid(-zf)
    q = h_q.astype(f32).reshape(bsz, s, HGRN_HEADS, HGRN_DK)
    v = h_i.astype(f32).reshape(bsz, s, HGRN_HEADS, HGRN_DV)
    o = hgrn2_chunkwise(q, k, v, log_f)
    on = rmsnorm(o, norm_g.reshape(HGRN_HEADS, HGRN_DV)).reshape(bsz, s, -1)
    return (on * jax.nn.silu(h_g.astype(f32))).astype(h_g.dtype)


def conformer_conv_branch(c_a, c_b, conv_w, conv_b, ln_g, ln_b):
    u = c_a * jax.nn.sigmoid(c_b)
    u = causal_depthwise_conv(u, conv_w, conv_b)
    return jax.nn.silu(layernorm(u, ln_g, ln_b))


def setup_inputs(seed: int = 0) -> dict:
    key = jax.random.key(seed)
    ks = jax.random.split(key, 24)
    f32 = jnp.float32

    def nrm(k, shape, scale):
        return scale * jax.random.normal(k, shape, f32)

    offs = _col_offsets()
    f_lo = offs[4]
    b_in = nrm(ks[3], (DEPTH, D_IN), 0.02)
    b_in = b_in.at[:, f_lo:f_lo + MLSTM_HEADS].add(
        jnp.linspace(MLSTM_F_BIAS_LO, MLSTM_F_BIAS_HI, MLSTM_HEADS, dtype=f32))
    qk_w = 2 * MLSTM_HEADS * MLSTM_DQK
    return {
        "x": jax.random.normal(ks[0], (BATCH, SEQ, D_MODEL), f32),
        "norm_mix_g": 1.0 + nrm(ks[1], (DEPTH, D_MODEL), 0.02),
        "w_in": nrm(ks[2], (DEPTH, D_MODEL, D_IN), D_MODEL ** -0.5),
        "b_in": b_in,
        "mlstm_conv_w": nrm(ks[4], (DEPTH, MLSTM_CONV_K, qk_w), MLSTM_CONV_K ** -0.5),
        "mlstm_conv_b": nrm(ks[5], (DEPTH, qk_w), 0.02),
        "mlstm_norm_g": 1.0 + nrm(ks[6], (DEPTH, MLSTM_HEADS * MLSTM_DV), 0.02),
        "hgrn_lb_logits": nrm(ks[7], (DEPTH, HGRN_HEADS * HGRN_DK), 0.1),
        "hgrn_norm_g": 1.0 + nrm(ks[8], (DEPTH, HGRN_HEADS * HGRN_DV), 0.02),
        "conv_w": nrm(ks[9], (DEPTH, CONV_K, CONV_CH), CONV_K ** -0.5),
        "conv_b": nrm(ks[10], (DEPTH, CONV_CH), 0.02),
        "conv_ln_g": 1.0 + nrm(ks[11], (DEPTH, CONV_CH), 0.02),
        "conv_ln_b": nrm(ks[12], (DEPTH, CONV_CH), 0.02),
        "w_branch_a": nrm(ks[13], (DEPTH, BRANCH_W, D_MODEL), BRANCH_W ** -0.5),
        "w_branch_b": nrm(ks[14], (DEPTH, BRANCH_W, D_MODEL), BRANCH_W ** -0.5),
        "w_branch_c": nrm(ks[15], (DEPTH, CONV_CH, D_MODEL), CONV_CH ** -0.5),
        "w_out": nrm(ks[16], (DEPTH, D_MODEL, D_MODEL), D_MODEL ** -0.5),
        "norm_ffn_g": 1.0 + nrm(ks[17], (DEPTH, D_MODEL), 0.02),
        "w_ffn_in": nrm(ks[18], (DEPTH, D_MODEL, 2 * D_FF), D_MODEL ** -0.5),
        "w_ffn_out": nrm(ks[19], (DEPTH, D_FF, D_MODEL), D_FF ** -0.5),
        "final_norm_g": 1.0 + nrm(ks[20], (D_MODEL,), 0.02),
    }


def reference(x, norm_mix_g, w_in, b_in, mlstm_conv_w, mlstm_conv_b, mlstm_norm_g,
              hgrn_lb_logits, hgrn_norm_g, conv_w, conv_b, conv_ln_g, conv_ln_b,
              w_branch_a, w_branch_b, w_branch_c, w_out, norm_ffn_g, w_ffn_in,
              w_ffn_out, final_norm_g):
    offs = _col_offsets()
    lb_p = jax.nn.softmax(hgrn_lb_logits.astype(jnp.float32), axis=0)
    lower_bounds = jnp.cumsum(lb_p, axis=0) - lb_p[0]
    for l in range(DEPTH):
        h = rmsnorm(x, norm_mix_g[l])
        z = h @ w_in[l] + b_in[l]
        (m_qk, m_v, m_o, m_i, m_f, h_f, h_q, h_i, h_g,
         c_a, c_b, gate) = jnp.split(z, offs[1:-1], axis=-1)
        y_a = mlstm_branch(m_qk, m_v, m_o, m_i, m_f, mlstm_conv_w[l], mlstm_conv_b[l], mlstm_norm_g[l])
        y_b = hgrn2_branch(h_f, h_q, h_i, h_g, lower_bounds[l], hgrn_norm_g[l])
        y_c = conformer_conv_branch(c_a, c_b, conv_w[l], conv_b[l], conv_ln_g[l], conv_ln_b[l])
        g_a, g_b, g_c = jnp.split(jax.nn.sigmoid(gate), N_BRANCH, axis=-1)
        merged = (g_a * (y_a @ w_branch_a[l]) + g_b * (y_b @ w_branch_b[l])
                  + g_c * (y_c @ w_branch_c[l]))
        x = x + merged @ w_out[l]
        h2 = rmsnorm(x, norm_ffn_g[l])
        ffn_g, ffn_u = jnp.split(h2 @ w_ffn_in[l], 2, axis=-1)
        x = x + (jax.nn.silu(ffn_g) * ffn_u) @ w_ffn_out[l]
    return rmsnorm(x, final_norm_g)
```

```python
import functools

import jax
import jax.numpy as jnp
from jax import lax
from jax.experimental import pallas as pl
from jax.experimental.pallas import tpu as pltpu

F32 = jnp.float32
BF16 = jnp.bfloat16

D_MODEL = 1024
BRANCH_W = D_MODEL // 2
N_HEADS = 4
MLSTM_DV = BRANCH_W // N_HEADS
MLSTM_DQK = MLSTM_DV // 2
MLSTM_CONV_K = 4
HGRN_DK = BRANCH_W // N_HEADS
CONV_K = 31
D_FF = 2816
EPS = 1e-6
NEG_BIG = -1e30
F_FLOOR = 1e-30

V7X_SUBLANES = 8
V7X_LANES = 128
V7X_VMEM_BYTES = 64 * 1024 * 1024

CHUNK = 64
MIX_TILE = 256
FFN_TILE = 512
QK_HALO = V7X_SUBLANES
U_HALO = 4 * V7X_SUBLANES
HGRN_SAFE_DECAY = 150.0

C_QK = 0
C_V = 512
C_O = 1024
C_HF = 1536
C_HQ = 2048
C_HI = 2560
C_HG = 3072
C_CA = 3584
C_CB = 4096
C_GATE = 4608
C_IF = 7680
D_IN_PACKED = C_IF + V7X_LANES


def _sigmoid_pair(z):
    e = jnp.exp(-jnp.abs(z))
    big = 1.0 / (1.0 + e)
    small = e / (1.0 + e)
    pos = z >= 0
    return jnp.where(pos, big, small), jnp.where(pos, small, big)


def _sigmoid(z):
    return _sigmoid_pair(z)[0]


def _silu(z):
    return z * _sigmoid(z)


def _log_sigmoid(z):
    return jnp.minimum(z, 0.0) - jnp.log(1.0 + jnp.exp(-jnp.abs(z)))


def _dot(a, b):
    return jnp.dot(a, b, preferred_element_type=F32)


def _split3(a):
    hi = a.astype(BF16)
    r1 = a - hi.astype(F32)
    mid = r1.astype(BF16)
    lo = (r1 - mid.astype(F32)).astype(BF16)
    return hi, mid, lo


def _cumsum_rows(tri, a):
    hi, mid, lo = _split3(a)
    return (_dot(tri, hi) + _dot(tri, mid)) + _dot(tri, lo)


def _cumsum_cols(a, tri_t):
    hi, mid, lo = _split3(a)
    return (_dot(hi, tri_t) + _dot(mid, tri_t)) + _dot(lo, tri_t)


def _rms_rows(y, g):
    return y * lax.rsqrt(jnp.mean(y * y, axis=-1, keepdims=True) + EPS) * g


def _mixer_kernel(layer, tile,
                  x_ref, ng_ref, w_ref, b_ref, mcw_ref, mcb_ref, mng_ref, lbl_ref, hng_ref,
                  cw_ref, cb_ref, lng_ref, lnb_ref, wa_ref, wb_ref, wc_ref, wo_ref,
                  o_ref,
                  qkbuf, ubuf, ct_ref, m_ref, st_ref, ya_ref, yb_ref, hq_ref, hk_ref, hb_ref):
    n_chunks = tile // CHUNK

    @pl.when(pl.program_id(1) == 0)
    def _():
        qkbuf[0:QK_HALO, :] = jnp.zeros((QK_HALO, BRANCH_W), F32)
        ubuf[0:U_HALO, :] = jnp.zeros((U_HALO, BRANCH_W), F32)
        ct_ref[...] = jnp.zeros_like(ct_ref)
        m_ref[...] = jnp.zeros_like(m_ref)
        st_ref[...] = jnp.zeros_like(st_ref)

    x = x_ref[...]
    h = _rms_rows(x, ng_ref[...]).astype(BF16)

    def proj(lo, hi):
        return _dot(h, w_ref[:, lo:hi]) + b_ref[:, lo:hi]

    row_i = lax.broadcasted_iota(jnp.int32, (CHUNK, CHUNK), 0)
    col_i = lax.broadcasted_iota(jnp.int32, (CHUNK, CHUNK), 1)
    causal = col_i <= row_i
    tri = causal.astype(BF16)
    tri_t = (row_i <= col_i).astype(BF16)

    qkbuf[QK_HALO:QK_HALO + tile, :] = proj(C_QK, C_QK + BRANCH_W)
    acc = jnp.broadcast_to(mcb_ref[...], (tile, BRANCH_W))
    for k in range(MLSTM_CONV_K):
        lo = QK_HALO - (MLSTM_CONV_K - 1) + k
        acc = acc + mcw_ref[k:k + 1, :] * qkbuf[lo:lo + tile, :]
    qk = _silu(acc)
    qkbuf[0:QK_HALO, :] = qkbuf[tile:tile + QK_HALO, :]

    zv = proj(C_V, C_V + BRANCH_W)
    og = _sigmoid(proj(C_O, C_O + BRANCH_W))
    zif = proj(C_IF, C_IF + V7X_LANES)
    lane = lax.broadcasted_iota(jnp.int32, (tile, V7X_LANES), 1)
    gates = jnp.where(lane < N_HEADS, zif, _log_sigmoid(zif))
    gates_t = gates.T
    ones_col = (lax.broadcasted_iota(jnp.int32, (CHUNK, MLSTM_DV), 1) == 0).astype(F32)

    for c in range(n_chunks):
        r0 = c * CHUNK
        g_c = gates[r0:r0 + CHUNK, :]
        gt_c = gates_t[:, r0:r0 + CHUNK]
        bc = _cumsum_rows(tri, g_c)
        br = _cumsum_cols(gt_c, tri_t)
        for hd in range(N_HEADS):
            fcol = N_HEADS + hd
            b_col = bc[:, fcol:fcol + 1]
            b_row = br[fcol:fcol + 1, :]
            li_col = g_c[:, hd:hd + 1]
            li_row = gt_c[hd:hd + 1, :]
            b_last = bc[CHUNK - 1:CHUNK, fcol:fcol + 1]
            m_st = m_ref[hd:hd + 1, 0:1]

            dlog = jnp.where(causal, b_col - b_row + li_row, NEG_BIG)
            g = b_col + m_st
            m_t = jnp.maximum(g, jnp.max(dlog, axis=-1, keepdims=True))
            w = jnp.where(causal, jnp.exp(dlog - m_t), 0.0)
            inter = jnp.exp(g - m_t)

            qh = (qk[r0:r0 + CHUNK, hd * MLSTM_DQK:(hd + 1) * MLSTM_DQK] * (MLSTM_DQK ** -0.5)).astype(BF16)
            kh = qk[r0:r0 + CHUNK, 256 + hd * MLSTM_DQK:256 + (hd + 1) * MLSTM_DQK]
            vh = zv[r0:r0 + CHUNK, hd * MLSTM_DV:(hd + 1) * MLSTM_DV]
            v_ext = jnp.concatenate([vh, ones_col], axis=1).astype(BF16)
            ct = ct_ref[hd]

            sc = pl.dot(qh, kh.astype(BF16), trans_b=True) * w
            nd = _dot(sc.astype(BF16), v_ext) + inter * _dot(qh, ct.astype(BF16))
            num = nd[:, 0:MLSTM_DV]
            den = nd[:, MLSTM_DV:MLSTM_DV + 1]
            h_out = num / jnp.maximum(jnp.abs(den), jnp.exp(-m_t))
            hn = _rms_rows(h_out, mng_ref[:, hd * MLSTM_DV:(hd + 1) * MLSTM_DV])
            ya_ref[r0:r0 + CHUNK, hd * MLSTM_DV:(hd + 1) * MLSTM_DV] = (
                og[r0:r0 + CHUNK, hd * MLSTM_DV:(hd + 1) * MLSTM_DV] * hn)

            dec_row = b_last - b_row + li_row
            m_new = jnp.maximum(b_last + m_st, jnp.max(dec_row, axis=-1, keepdims=True))
            ws = jnp.exp(b_last - b_col + li_col - m_new)
            keep = jnp.exp(b_last + m_st - m_new)
            ct_ref[hd] = keep * ct + pl.dot((ws * kh).astype(BF16), v_ext, trans_a=True)
            m_ref[hd:hd + 1, 0:1] = m_new

    lrows = [lbl_ref[j:j + 1, :] for j in range(lbl_ref.shape[0])]
    lmax = functools.reduce(jnp.maximum, lrows)
    lexp = [jnp.exp(r - lmax) for r in lrows]
    lsum = functools.reduce(jnp.add, lexp)
    lb = jnp.zeros_like(lmax)
    for j in range(1, layer + 1):
        lb = lb + lexp[j] / lsum

    sg_pos, sg_neg = _sigmoid_pair(proj(C_HF, C_HF + BRANCH_W))
    logf = jnp.log(jnp.maximum(lb + (1.0 - lb) * sg_pos, F_FLOOR))
    hk = (1.0 - lb) * sg_neg
    hq = proj(C_HQ, C_HQ + BRANCH_W) * (HGRN_DK ** -0.5)
    hv = proj(C_HI, C_HI + BRANCH_W).astype(BF16)
    hgate = _silu(proj(C_HG, C_HG + BRANCH_W))

    b_chunks = [_cumsum_rows(tri, logf[c * CHUNK:(c + 1) * CHUNK, :]) for c in range(n_chunks)]
    worst = functools.reduce(jnp.minimum, [b[CHUNK - 1:CHUNK, :] for b in b_chunks])
    unsafe = jnp.min(worst) < -HGRN_SAFE_DECAY

    def hgrn_chunks(intra_fn):
        for c in range(n_chunks):
            r0 = c * CHUNK
            b = b_chunks[c]
            b_last = b[CHUNK - 1:CHUNK, :]
            q_c = hq[r0:r0 + CHUNK, :]
            k_c = hk[r0:r0 + CHUNK, :]
            q_hat = (q_c * jnp.exp(b)).astype(BF16)
            k_hat = (k_c * jnp.exp(b_last - b)).astype(BF16)
            decay = jnp.exp(b_last)
            a_heads = intra_fn(c, b, b_last, q_c, k_c)
            for hd in range(N_HEADS):
                hs = slice(hd * HGRN_DK, (hd + 1) * HGRN_DK)
                v_h = hv[r0:r0 + CHUNK, hs]
                st = st_ref[hd]
                o = _dot(a_heads[hd].astype(BF16), v_h) + pl.dot(q_hat[:, hs], st.astype(BF16), trans_b=True)
                st_ref[hd] = st * decay[:, hs] + pl.dot(v_h, k_hat[:, hs], trans_a=True)
                yb_ref[r0:r0 + CHUNK, hs] = _rms_rows(o, hng_ref[:, hs]) * hgate[r0:r0 + CHUNK, hs]

    def intra_factorised(c, b, b_last, q_c, k_c):
        r = 0.5 * b_last
        q_t = (q_c * jnp.exp(b - r)).astype(BF16)
        k_t = (k_c * jnp.exp(r - b)).astype(BF16)
        out = []
        for hd in range(N_HEADS):
            hs = slice(hd * HGRN_DK, (hd + 1) * HGRN_DK)
            out.append(jnp.where(causal, pl.dot(q_t[:, hs], k_t[:, hs], trans_b=True), 0.0))
        return out

    def intra_direct(c, b, b_last, q_c, k_c):
        hq_ref[...] = q_c
        hk_ref[...] = k_c
        hb_ref[...] = b
        out = []
        for hd in range(N_HEADS):
            hs = slice(hd * HGRN_DK, (hd + 1) * HGRN_DK)

            def body(grp, a, hs=hs):
                g0 = pl.multiple_of(grp * V7X_SUBLANES, V7X_SUBLANES)
                k_g = hk_ref[pl.ds(g0, V7X_SUBLANES), hs]
                b_g = hb_ref[pl.ds(g0, V7X_SUBLANES), hs]
                for j in range(V7X_SUBLANES):
                    e = hq_ref[:, hs] * k_g[j:j + 1, :] * jnp.exp(
                        jnp.minimum(hb_ref[:, hs] - b_g[j:j + 1, :], 0.0))
                    a = jnp.where(col_i == g0 + j, jnp.sum(e, axis=-1, keepdims=True), a)
                return a

            a = lax.fori_loop(0, CHUNK // V7X_SUBLANES, body, jnp.zeros((CHUNK, CHUNK), F32))
            out.append(jnp.where(causal, a, 0.0))
        return out

    @pl.when(jnp.logical_not(unsafe))
    def _():
        hgrn_chunks(intra_factorised)

    @pl.when(unsafe)
    def _():
        hgrn_chunks(intra_direct)

    ubuf[U_HALO:U_HALO + tile, :] = proj(C_CA, C_CA + BRANCH_W) * _sigmoid(proj(C_CB, C_CB + BRANCH_W))
    acc = jnp.broadcast_to(cb_ref[...], (tile, BRANCH_W))
    for k in range(CONV_K):
        lo = U_HALO - (CONV_K - 1) + k
        acc = acc + cw_ref[k:k + 1, :] * ubuf[lo:lo + tile, :]
    ubuf[0:U_HALO, :] = ubuf[tile:tile + U_HALO, :]
    mu = jnp.mean(acc, axis=-1, keepdims=True)
    cen = acc - mu
    var = jnp.mean(cen * cen, axis=-1, keepdims=True)
    yc = _silu(cen * lax.rsqrt(var + EPS) * lng_ref[...] + lnb_ref[...])

    merged = _sigmoid(proj(C_GATE, C_GATE + D_MODEL)) * _dot(ya_ref[...].astype(BF16), wa_ref[...])
    merged = merged + _sigmoid(proj(C_GATE + D_MODEL, C_GATE + 2 * D_MODEL)) * _dot(
        yb_ref[...].astype(BF16), wb_ref[...])
    merged = merged + _sigmoid(proj(C_GATE + 2 * D_MODEL, C_GATE + 3 * D_MODEL)) * _dot(
        yc.astype(BF16), wc_ref[...])
    o_ref[...] = x + _dot(merged.astype(BF16), wo_ref[...])


def _ffn_kernel(final, x_ref, g_ref, wi_ref, wo_ref, fg_ref, o_ref):
    x = x_ref[...]
    h = _rms_rows(x, g_ref[...]).astype(BF16)
    gate = _dot(h, wi_ref[:, 0:D_FF])
    up = _dot(h, wi_ref[:, D_FF:2 * D_FF])
    y = x + _dot((_silu(gate) * up).astype(BF16), wo_ref[...])
    if final:
        y = _rms_rows(y, fg_ref[...])
    o_ref[...] = y


def _resident(shape, index):
    return pl.BlockSpec(shape, index, pipeline_mode=pl.Buffered(1))


def _mixer_call(layer, x, ng, w_mix, b_mix, mcw, mcb, mng, lbl, hng, cw, cb, lng, lnb, wa, wb, wc, wo):
    bsz, seq, d = x.shape
    tile = MIX_TILE
    assert seq % tile == 0 and tile % CHUNK == 0 and d == D_MODEL
    depth = lbl.shape[0]

    def lay3(shape):
        return _resident((None,) + shape, lambda b, t: (layer, 0, 0))

    in_specs = [
        pl.BlockSpec((None, tile, d), lambda b, t: (b, t, 0)),
        lay3((1, d)),
        lay3((d, D_IN_PACKED)),
        lay3((1, D_IN_PACKED)),
        lay3((MLSTM_CONV_K, BRANCH_W)),
        lay3((1, BRANCH_W)),
        lay3((1, BRANCH_W)),
        _resident((depth, BRANCH_W), lambda b, t: (0, 0)),
        lay3((1, BRANCH_W)),
        lay3((CONV_K, BRANCH_W)),
        lay3((1, BRANCH_W)),
        lay3((1, BRANCH_W)),
        lay3((1, BRANCH_W)),
        lay3((BRANCH_W, d)),
        lay3((BRANCH_W, d)),
        lay3((BRANCH_W, d)),
        lay3((d, d)),
    ]
    scratch = [
        pltpu.VMEM((tile + QK_HALO, BRANCH_W), F32),
        pltpu.VMEM((tile + U_HALO, BRANCH_W), F32),
        pltpu.VMEM((N_HEADS, MLSTM_DQK, 2 * MLSTM_DV), F32),
        pltpu.VMEM((V7X_SUBLANES, V7X_LANES), F32),
        pltpu.VMEM((N_HEADS, HGRN_DK, HGRN_DK), F32),
        pltpu.VMEM((tile, BRANCH_W), F32),
        pltpu.VMEM((tile, BRANCH_W), F32),
        pltpu.VMEM((CHUNK, BRANCH_W), F32),
        pltpu.VMEM((CHUNK, BRANCH_W), F32),
        pltpu.VMEM((CHUNK, BRANCH_W), F32),
    ]
    return pl.pallas_call(
        functools.partial(_mixer_kernel, layer, tile),
        grid=(bsz, seq // tile),
        in_specs=in_specs,
        out_specs=pl.BlockSpec((None, tile, d), lambda b, t: (b, t, 0)),
        out_shape=jax.ShapeDtypeStruct(x.shape, x.dtype),
        scratch_shapes=scratch,
        compiler_params=pltpu.CompilerParams(
            dimension_semantics=("arbitrary", "arbitrary"),
            vmem_limit_bytes=(V7X_VMEM_BYTES * 3) // 4),
        name=f"mixer_l{layer}",
    )(x, ng, w_mix, b_mix, mcw, mcb, mng, lbl, hng, cw, cb, lng, lnb, wa, wb, wc, wo)


def _ffn_call(layer, final, x2, g, wi, wo, fg):
    n, d = x2.shape
    tile = FFN_TILE
    assert n % tile == 0

    def lay3(shape):
        return _resident((None,) + shape, lambda i: (layer, 0, 0))

    return pl.pallas_call(
        functools.partial(_ffn_kernel, final),
        grid=(n // tile,),
        in_specs=[
            pl.BlockSpec((tile, d), lambda i: (i, 0)),
            lay3((1, d)),
            lay3((d, 2 * D_FF)),
            lay3((D_FF, d)),
            _resident((1, d), lambda i: (0, 0)),
        ],
        out_specs=pl.BlockSpec((tile, d), lambda i: (i, 0)),
        out_shape=jax.ShapeDtypeStruct(x2.shape, x2.dtype),
        compiler_params=pltpu.CompilerParams(
            dimension_semantics=("arbitrary",),
            vmem_limit_bytes=(V7X_VMEM_BYTES * 3) // 4),
        name=f"ffn_l{layer}",
    )(x2, g, wi, wo, fg)


def kernel(x, norm_mix_g, w_in, b_in, mlstm_conv_w, mlstm_conv_b, mlstm_norm_g, hgrn_lb_logits, hgrn_norm_g,
           conv_w, conv_b, conv_ln_g, conv_ln_b, w_branch_a, w_branch_b, w_branch_c, w_out, norm_ffn_g,
           w_ffn_in, w_ffn_out, final_norm_g):
    depth = w_in.shape[0]
    bsz, seq, d = x.shape
    n_small = 2 * N_HEADS
    gate_lo = 3 * BRANCH_W
    pad = V7X_LANES - n_small

    def pack_cols(a):
        return jnp.concatenate(
            [a[..., :gate_lo], a[..., gate_lo + n_small:], a[..., gate_lo:gate_lo + n_small],
             jnp.zeros(a.shape[:-1] + (pad,), a.dtype)], axis=-1)

    w_mix = pack_cols(w_in).astype(BF16)
    b_mix = pack_cols(b_in)[:, None, :]
    row = lambda a: a[:, None, :]
    wa = w_branch_a.astype(BF16)
    wb = w_branch_b.astype(BF16)
    wc = w_branch_c.astype(BF16)
    wo = w_out.astype(BF16)
    wfi = w_ffn_in.astype(BF16)
    wfo = w_ffn_out.astype(BF16)
    fg = final_norm_g[None, :]

    for layer in range(depth):
        x = _mixer_call(layer, x, row(norm_mix_g), w_mix, b_mix, mlstm_conv_w, row(mlstm_conv_b),
                        row(mlstm_norm_g), hgrn_lb_logits, row(hgrn_norm_g), conv_w, row(conv_b),
                        row(conv_ln_g), row(conv_ln_b), wa, wb, wc, wo)
        x = _ffn_call(layer, layer == depth - 1, x.reshape(bsz * seq, d), row(norm_ffn_g), wfi, wfo,
                      fg).reshape(bsz, seq, d)
    return x
```

```python
import functools

import jax
import jax.numpy as jnp
from jax import lax
from jax.experimental import pallas as pl
from jax.experimental.pallas import tpu as pltpu

F32 = jnp.float32
BF16 = jnp.bfloat16

D_MODEL = 1024
BRANCH_W = D_MODEL // 2
N_HEADS = 4
MLSTM_DV = BRANCH_W // N_HEADS
MLSTM_DQK = MLSTM_DV // 2
MLSTM_CONV_K = 4
HGRN_DK = BRANCH_W // N_HEADS
CONV_K = 31
D_FF = 2816
EPS = 1e-6
F_FLOOR = 1e-30

V7X_SUBLANES = 8
V7X_LANES = 128
V7X_VMEM_BYTES = 64 * 1024 * 1024

CHUNK = 64
MIX_TILE = 512
FFN_TILE = 512
QK_HALO = V7X_SUBLANES
U_HALO = 4 * V7X_SUBLANES
HGRN_SAFE_DECAY = 150.0

C_QK = 0
C_V = 512
C_O = 1024
C_HF = 1536
C_HQ = 2048
C_HI = 2560
C_HG = 3072
C_CA = 3584
C_CB = 4096
C_GATE = 4608
C_GI = 7680
C_GF = C_GI + V7X_LANES
D_IN_PACKED = C_GF + V7X_LANES


def _sigmoid_pair(z):
    e = jnp.exp(-jnp.abs(z))
    big = 1.0 / (1.0 + e)
    small = e / (1.0 + e)
    pos = z >= 0
    return jnp.where(pos, big, small), jnp.where(pos, small, big)


def _sigmoid(z):
    return 0.5 * jnp.tanh(0.5 * z) + 0.5


def _silu(z):
    return z * _sigmoid(z)


def _log_sigmoid(z):
    return jnp.minimum(z, 0.0) - jnp.log(1.0 + jnp.exp(-jnp.abs(z)))


def _dot(a, b):
    return jnp.dot(a, b, preferred_element_type=F32)


def _split3(a):
    hi = a.astype(BF16)
    r1 = a - hi.astype(F32)
    mid = r1.astype(BF16)
    lo = (r1 - mid.astype(F32)).astype(BF16)
    return hi, mid, lo


def _cumsum_rows(tri, a):
    hi, mid, lo = _split3(a)
    return (_dot(tri, hi) + _dot(tri, mid)) + _dot(tri, lo)


def _rms_rows(y, g):
    return y * lax.rsqrt(jnp.mean(y * y, axis=-1, keepdims=True) + EPS) * g


def _causal_depthwise_conv(buf, w_ref, bias, tile, halo):
    ksize = w_ref.shape[0]
    ext = tile + V7X_SUBLANES
    acc = jnp.broadcast_to(bias, (tile, buf.shape[1]))
    for res in range(min(V7X_SUBLANES, ksize)):
        part = None
        for d in range(res, ksize, V7X_SUBLANES):
            lo = halo - V7X_SUBLANES - (d - res)
            term = w_ref[ksize - 1 - d:ksize - d, :] * buf[lo:lo + ext, :]
            part = term if part is None else part + term
        acc = acc + part[V7X_SUBLANES - res:V7X_SUBLANES - res + tile, :]
    return acc


def _mixer_kernel(layer, tile,
                  x_ref, ng_ref, w_ref, b_ref, mcw_ref, mcb_ref, mng_ref, lbl_ref, hng_ref,
                  cw_ref, cb_ref, lng_ref, lnb_ref, wa_ref, wb_ref, wc_ref, wo_ref,
                  o_ref,
                  qkbuf, ubuf, ct_ref, m_ref, st_ref, ya_ref, yb_ref, a_ref, hq_ref, hk_ref, hb_ref):
    n_chunks = tile // CHUNK

    @pl.when(pl.program_id(1) == 0)
    def _():
        qkbuf[0:QK_HALO, :] = jnp.zeros((QK_HALO, BRANCH_W), F32)
        ubuf[0:U_HALO, :] = jnp.zeros((U_HALO, BRANCH_W), F32)
        ct_ref[...] = jnp.zeros_like(ct_ref)
        m_ref[...] = jnp.zeros_like(m_ref)
        st_ref[...] = jnp.zeros_like(st_ref)

    x = x_ref[...]
    h = _rms_rows(x, ng_ref[...]).astype(BF16)

    def proj(lo, hi):
        return _dot(h, w_ref[:, lo:hi]) + b_ref[:, lo:hi]

    row_i = lax.broadcasted_iota(jnp.int32, (CHUNK, CHUNK), 0)
    col_i = lax.broadcasted_iota(jnp.int32, (CHUNK, CHUNK), 1)
    causal = col_i <= row_i
    tri = causal.astype(BF16)

    qkbuf[QK_HALO:QK_HALO + tile, :] = proj(C_QK, C_QK + BRANCH_W)
    qk = _silu(_causal_depthwise_conv(qkbuf, mcw_ref, mcb_ref[...], tile, QK_HALO))
    qkbuf[0:QK_HALO, :] = qkbuf[tile:tile + QK_HALO, :]

    zv = proj(C_V, C_V + BRANCH_W)
    og = _sigmoid(proj(C_O, C_O + BRANCH_W))
    li_all = proj(C_GI, C_GI + V7X_LANES)
    lf_all = _log_sigmoid(proj(C_GF, C_GF + V7X_LANES))
    lane_c = lax.broadcasted_iota(jnp.int32, (CHUNK, V7X_LANES), 1)
    row_c = lax.broadcasted_iota(jnp.int32, (CHUNK, V7X_LANES), 0)
    lane_grp = lane_c // V7X_SUBLANES
    lane_head = lane_c % V7X_SUBLANES
    ones_blk = jnp.ones((CHUNK, MLSTM_DV), F32)

    def place3(a, first_grp):
        hi, mid, lo = _split3(a)
        zero = jnp.zeros_like(hi)
        return jnp.where(lane_grp == first_grp, hi,
                         jnp.where(lane_grp == first_grp + 1, mid,
                                   jnp.where(lane_grp == first_grp + 2, lo, zero)))

    ones_lo = (lane_grp < 3).astype(BF16)
    ones_hi = jnp.logical_and(lane_grp >= 3, lane_grp < 6).astype(BF16)

    bcs, cs, mus, m_prevs, mu_lasts = [], [], [], [], []
    m_row = m_ref[0:1, :]
    for c in range(n_chunks):
        r0 = c * CHUNK
        bc = _cumsum_rows(tri, lf_all[r0:r0 + CHUNK, :])
        cc = li_all[r0:r0 + CHUNK, :] - bc
        cm = cc
        for sh in (1, 2, 4, 8, 16, 32):
            cm = jnp.where(row_c >= sh, jnp.maximum(cm, pltpu.roll(cm, sh, axis=0)), cm)
        mu_last = jnp.maximum(m_row, cm[CHUNK - 1:CHUNK, :])
        bcs.append(bc)
        cs.append(cc)
        mus.append(jnp.maximum(m_row, cm))
        m_prevs.append(m_row)
        mu_lasts.append(mu_last)
        m_row = bc[CHUNK - 1:CHUNK, :] + mu_last
    m_ref[0:1, :] = m_row

    ct_st = [ct_ref[hd] for hd in range(N_HEADS)]
    for c in range(n_chunks):
        r0 = c * CHUNK
        bc, cc, mu, m_prev, mu_last = bcs[c], cs[c], mus[c], m_prevs[c], mu_lasts[c]
        inter_all = jnp.exp(m_prev - mu)
        rinv_all = jnp.exp(-(bc + mu))
        ws_all = jnp.exp(cc - mu_last)
        keep_all = jnp.exp(m_prev - mu_last)
        lhs = place3(-mu, 0) + ones_hi
        rhs = ones_lo + place3(cc, 3)
        for hd in range(N_HEADS):
            rhs_h = jnp.where(lane_head == hd, rhs, jnp.zeros_like(rhs))
            w = jnp.where(causal, jnp.exp(pl.dot(lhs, rhs_h, trans_b=True)), 0.0)
            inter = inter_all[:, hd:hd + 1]

            qh = qk[r0:r0 + CHUNK, hd * MLSTM_DQK:(hd + 1) * MLSTM_DQK] * (MLSTM_DQK ** -0.5)
            kh = qk[r0:r0 + CHUNK, 256 + hd * MLSTM_DQK:256 + (hd + 1) * MLSTM_DQK]
            vh = zv[r0:r0 + CHUNK, hd * MLSTM_DV:(hd + 1) * MLSTM_DV]
            v_ext = jnp.concatenate([vh, ones_blk], axis=1).astype(BF16)
            ct = ct_st[hd]

            sc = pl.dot(qh.astype(BF16), kh.astype(BF16), trans_b=True) * w
            nd = _dot(sc.astype(BF16), v_ext) + _dot((qh * inter).astype(BF16), ct.astype(BF16))
            num = nd[:, 0:MLSTM_DV]
            den = nd[:, MLSTM_DV:2 * MLSTM_DV]
            h_out = num / jnp.maximum(jnp.abs(den), rinv_all[:, hd:hd + 1])
            hn = _rms_rows(h_out, mng_ref[:, hd * MLSTM_DV:(hd + 1) * MLSTM_DV])
            ya_ref[r0:r0 + CHUNK, hd * MLSTM_DV:(hd + 1) * MLSTM_DV] = (
                og[r0:r0 + CHUNK, hd * MLSTM_DV:(hd + 1) * MLSTM_DV] * hn)

            wk = (ws_all[:, hd:hd + 1] * kh).astype(BF16)
            ct_st[hd] = keep_all[:, hd:hd + 1] * ct + pl.dot(wk, v_ext, trans_a=True)

    for hd in range(N_HEADS):
        ct_ref[hd] = ct_st[hd]

    lrows = [lbl_ref[j:j + 1, :] for j in range(lbl_ref.shape[0])]
    lmax = functools.reduce(jnp.maximum, lrows)
    lexp = [jnp.exp(r - lmax) for r in lrows]
    lsum = functools.reduce(jnp.add, lexp)
    lb = jnp.zeros_like(lmax)
    for j in range(1, layer + 1):
        lb = lb + lexp[j] / lsum

    sg_pos, sg_neg = _sigmoid_pair(proj(C_HF, C_HF + BRANCH_W))
    logf = jnp.log(jnp.maximum(lb + (1.0 - lb) * sg_pos, F_FLOOR))
    hk = (1.0 - lb) * sg_neg
    hq = proj(C_HQ, C_HQ + BRANCH_W) * (HGRN_DK ** -0.5)
    hv = proj(C_HI, C_HI + BRANCH_W).astype(BF16)
    hgate = _silu(proj(C_HG, C_HG + BRANCH_W))

    b_chunks = [_cumsum_rows(tri, logf[c * CHUNK:(c + 1) * CHUNK, :]) for c in range(n_chunks)]
    worst = functools.reduce(jnp.minimum, [b[CHUNK - 1:CHUNK, :] for b in b_chunks])
    unsafe = jnp.min(worst) < -HGRN_SAFE_DECAY

    @pl.when(jnp.logical_not(unsafe))
    def _():
        for c in range(n_chunks):
            r0 = c * CHUNK
            b = b_chunks[c]
            r = 0.5 * b[CHUNK - 1:CHUNK, :]
            q_t = (hq[r0:r0 + CHUNK, :] * jnp.exp(b - r)).astype(BF16)
            k_t = (hk[r0:r0 + CHUNK, :] * jnp.exp(r - b)).astype(BF16)
            for hd in range(N_HEADS):
                hs = slice(hd * HGRN_DK, (hd + 1) * HGRN_DK)
                a = pl.dot(q_t[:, hs], k_t[:, hs], trans_b=True)
                a_ref[c * N_HEADS + hd] = jnp.where(causal, a, 0.0).astype(BF16)

    @pl.when(unsafe)
    def _():
        for c in range(n_chunks):
            r0 = c * CHUNK
            hq_ref[...] = hq[r0:r0 + CHUNK, :]
            hk_ref[...] = hk[r0:r0 + CHUNK, :]
            hb_ref[...] = b_chunks[c]
            for hd in range(N_HEADS):
                hs = slice(hd * HGRN_DK, (hd + 1) * HGRN_DK)

                def body(grp, a, hs=hs):
                    g0 = pl.multiple_of(grp * V7X_SUBLANES, V7X_SUBLANES)
                    k_g = hk_ref[pl.ds(g0, V7X_SUBLANES), hs]
                    b_g = hb_ref[pl.ds(g0, V7X_SUBLANES), hs]
                    for j in range(V7X_SUBLANES):
                        e = hq_ref[:, hs] * k_g[j:j + 1, :] * jnp.exp(
                            jnp.minimum(hb_ref[:, hs] - b_g[j:j + 1, :], 0.0))
                        a = jnp.where(col_i == g0 + j, jnp.sum(e, axis=-1, keepdims=True), a)
                    return a

                a = lax.fori_loop(0, CHUNK // V7X_SUBLANES, body, jnp.zeros((CHUNK, CHUNK), F32))
                a_ref[c * N_HEADS + hd] = jnp.where(causal, a, 0.0).astype(BF16)

    s_st = [st_ref[hd] for hd in range(N_HEADS)]
    for c in range(n_chunks):
        r0 = c * CHUNK
        b = b_chunks[c]
        b_last = b[CHUNK - 1:CHUNK, :]
        q_hat = (hq[r0:r0 + CHUNK, :] * jnp.exp(b)).astype(BF16)
        k_hat = (hk[r0:r0 + CHUNK, :] * jnp.exp(b_last - b)).astype(BF16)
        decay = jnp.exp(b_last)
        for hd in range(N_HEADS):
            hs = slice(hd * HGRN_DK, (hd + 1) * HGRN_DK)
            v_h = hv[r0:r0 + CHUNK, hs]
            st = s_st[hd]
            o = _dot(a_ref[c * N_HEADS + hd], v_h) + pl.dot(q_hat[:, hs], st.astype(BF16), trans_b=True)
            s_st[hd] = st * decay[:, hs] + pl.dot(v_h, k_hat[:, hs], trans_a=True)
            yb_ref[r0:r0 + CHUNK, hs] = _rms_rows(o, hng_ref[:, hs]) * hgate[r0:r0 + CHUNK, hs]
    for hd in range(N_HEADS):
        st_ref[hd] = s_st[hd]

    ubuf[U_HALO:U_HALO + tile, :] = proj(C_CA, C_CA + BRANCH_W) * _sigmoid(proj(C_CB, C_CB + BRANCH_W))
    acc = _causal_depthwise_conv(ubuf, cw_ref, cb_ref[...], tile, U_HALO)
    ubuf[0:U_HALO, :] = ubuf[tile:tile + U_HALO, :]
    mu = jnp.mean(acc, axis=-1, keepdims=True)
    cen = acc - mu
    var = jnp.mean(cen * cen, axis=-1, keepdims=True)
    yc = _silu(cen * lax.rsqrt(var + EPS) * lng_ref[...] + lnb_ref[...])

    merged = _sigmoid(proj(C_GATE, C_GATE + D_MODEL)) * _dot(ya_ref[...].astype(BF16), wa_ref[...])
    merged = merged + _sigmoid(proj(C_GATE + D_MODEL, C_GATE + 2 * D_MODEL)) * _dot(
        yb_ref[...].astype(BF16), wb_ref[...])
    merged = merged + _sigmoid(proj(C_GATE + 2 * D_MODEL, C_GATE + 3 * D_MODEL)) * _dot(
        yc.astype(BF16), wc_ref[...])
    o_ref[...] = x + _dot(merged.astype(BF16), wo_ref[...])


def _ffn_kernel(final, x_ref, g_ref, wi_ref, wo_ref, fg_ref, o_ref):
    x = x_ref[...]
    h = _rms_rows(x, g_ref[...]).astype(BF16)
    gate = _dot(h, wi_ref[:, 0:D_FF])
    up = _dot(h, wi_ref[:, D_FF:2 * D_FF])
    y = x + _dot((_silu(gate) * up).astype(BF16), wo_ref[...])
    if final:
        y = _rms_rows(y, fg_ref[...])
    o_ref[...] = y


def _resident(shape, index):
    return pl.BlockSpec(shape, index, pipeline_mode=pl.Buffered(1))


def _mixer_call(layer, x, ng, w_mix, b_mix, mcw, mcb, mng, lbl, hng, cw, cb, lng, lnb, wa, wb, wc, wo):
    bsz, seq, d = x.shape
    tile = MIX_TILE
    assert seq % tile == 0 and tile % CHUNK == 0 and d == D_MODEL
    depth = lbl.shape[0]

    def lay3(shape):
        return _resident((None,) + shape, lambda b, t: (layer, 0, 0))

    in_specs = [
        pl.BlockSpec((None, tile, d), lambda b, t: (b, t, 0)),
        lay3((1, d)),
        lay3((d, D_IN_PACKED)),
        lay3((1, D_IN_PACKED)),
        lay3((MLSTM_CONV_K, BRANCH_W)),
        lay3((1, BRANCH_W)),
        lay3((1, BRANCH_W)),
        _resident((depth, BRANCH_W), lambda b, t: (0, 0)),
        lay3((1, BRANCH_W)),
        lay3((CONV_K, BRANCH_W)),
        lay3((1, BRANCH_W)),
        lay3((1, BRANCH_W)),
        lay3((1, BRANCH_W)),
        lay3((BRANCH_W, d)),
        lay3((BRANCH_W, d)),
        lay3((BRANCH_W, d)),
        lay3((d, d)),
    ]
    scratch = [
        pltpu.VMEM((tile + QK_HALO, BRANCH_W), F32),
        pltpu.VMEM((tile + U_HALO, BRANCH_W), F32),
        pltpu.VMEM((N_HEADS, MLSTM_DQK, 2 * MLSTM_DV), F32),
        pltpu.VMEM((V7X_SUBLANES, V7X_LANES), F32),
        pltpu.VMEM((N_HEADS, HGRN_DK, HGRN_DK), F32),
        pltpu.VMEM((tile, BRANCH_W), F32),
        pltpu.VMEM((tile, BRANCH_W), F32),
        pltpu.VMEM(((tile // CHUNK) * N_HEADS, CHUNK, CHUNK), BF16),
        pltpu.VMEM((CHUNK, BRANCH_W), F32),
        pltpu.VMEM((CHUNK, BRANCH_W), F32),
        pltpu.VMEM((CHUNK, BRANCH_W), F32),
    ]
    return pl.pallas_call(
        functools.partial(_mixer_kernel, layer, tile),
        grid=(bsz, seq // tile),
        in_specs=in_specs,
        out_specs=pl.BlockSpec((None, tile, d), lambda b, t: (b, t, 0)),
        out_shape=jax.ShapeDtypeStruct(x.shape, x.dtype),
        scratch_shapes=scratch,
        compiler_params=pltpu.CompilerParams(
            dimension_semantics=("arbitrary", "arbitrary"),
            vmem_limit_bytes=(V7X_VMEM_BYTES * 3) // 4),
        name=f"mixer_l{layer}",
    )(x, ng, w_mix, b_mix, mcw, mcb, mng, lbl, hng, cw, cb, lng, lnb, wa, wb, wc, wo)


def _ffn_call(layer, final, x2, g, wi, wo, fg):
    n, d = x2.shape
    tile = FFN_TILE
    assert n % tile == 0

    def lay3(shape):
        return _resident((None,) + shape, lambda i: (layer, 0, 0))

    return pl.pallas_call(
        functools.partial(_ffn_kernel, final),
        grid=(n // tile,),
        in_specs=[
            pl.BlockSpec((tile, d), lambda i: (i, 0)),
            lay3((1, d)),
            lay3((d, 2 * D_FF)),
            lay3((D_FF, d)),
            _resident((1, d), lambda i: (0, 0)),
        ],
        out_specs=pl.BlockSpec((tile, d), lambda i: (i, 0)),
        out_shape=jax.ShapeDtypeStruct(x2.shape, x2.dtype),
        compiler_params=pltpu.CompilerParams(
            dimension_semantics=("arbitrary",),
            vmem_limit_bytes=(V7X_VMEM_BYTES * 3) // 4),
        name=f"ffn_l{layer}",
    )(x2, g, wi, wo, fg)


def kernel(x, norm_mix_g, w_in, b_in, mlstm_conv_w, mlstm_conv_b, mlstm_norm_g, hgrn_lb_logits, hgrn_norm_g,
           conv_w, conv_b, conv_ln_g, conv_ln_b, w_branch_a, w_branch_b, w_branch_c, w_out, norm_ffn_g,
           w_ffn_in, w_ffn_out, final_norm_g):
    depth = w_in.shape[0]
    bsz, seq, d = x.shape
    n_small = 2 * N_HEADS
    gate_lo = 3 * BRANCH_W
    n_grp = V7X_LANES // V7X_SUBLANES

    def replicate(g):
        g = jnp.concatenate([g, jnp.zeros(g.shape[:-1] + (V7X_SUBLANES - N_HEADS,), g.dtype)], axis=-1)
        return jnp.tile(g, (1,) * (g.ndim - 1) + (n_grp,))

    def pack_cols(a):
        return jnp.concatenate(
            [a[..., :gate_lo], a[..., gate_lo + n_small:],
             replicate(a[..., gate_lo:gate_lo + N_HEADS]),
             replicate(a[..., gate_lo + N_HEADS:gate_lo + n_small])], axis=-1)

    w_mix = pack_cols(w_in).astype(BF16)
    b_mix = pack_cols(b_in)[:, None, :]
    row = lambda a: a[:, None, :]
    wa = w_branch_a.astype(BF16)
    wb = w_branch_b.astype(BF16)
    wc = w_branch_c.astype(BF16)
    wo = w_out.astype(BF16)
    wfi = w_ffn_in.astype(BF16)
    wfo = w_ffn_out.astype(BF16)
    fg = final_norm_g[None, :]

    for layer in range(depth):
        x = _mixer_call(layer, x, row(norm_mix_g), w_mix, b_mix, mlstm_conv_w, row(mlstm_conv_b),
                        row(mlstm_norm_g), hgrn_lb_logits, row(hgrn_norm_g), conv_w, row(conv_b),
                        row(conv_ln_g), row(conv_ln_b), wa, wb, wc, wo)
        x = _ffn_call(layer, layer == depth - 1, x.reshape(bsz * seq, d), row(norm_ffn_g), wfi, wfo,
                      fg).reshape(bsz, seq, d)
    return x
```

```python
import functools

import jax
import jax.numpy as jnp
from jax import lax
from jax.experimental import pallas as pl
from jax.experimental.pallas import tpu as pltpu

F32 = jnp.float32
BF16 = jnp.bfloat16

D_MODEL = 1024
BRANCH_W = D_MODEL // 2
N_HEADS = 4
MLSTM_DV = BRANCH_W // N_HEADS
MLSTM_DQK = MLSTM_DV // 2
MLSTM_CONV_K = 4
HGRN_DK = BRANCH_W // N_HEADS
CONV_K = 31
D_FF = 2816
EPS = 1e-6
F_FLOOR = 1e-30

V7X_SUBLANES = 8
V7X_LANES = 128
V7X_VMEM_BYTES = 64 * 1024 * 1024

CHUNK = 128
MIX_TILE = 512
FFN_TILE = 512
QK_HALO = V7X_SUBLANES
U_HALO = 4 * V7X_SUBLANES
HGRN_SAFE_DECAY = 150.0

C_QK = 0
C_V = 512
C_O = 1024
C_HF = 1536
C_HQ = 2048
C_HI = 2560
C_HG = 3072
C_CA = 3584
C_CB = 4096
C_GATE = 4608
C_GI = 7680
C_GF = C_GI + V7X_LANES
D_IN_PACKED = C_GF + V7X_LANES


def _sigmoid_pair(z):
    e = jnp.exp(-jnp.abs(z))
    big = 1.0 / (1.0 + e)
    small = e / (1.0 + e)
    pos = z >= 0
    return jnp.where(pos, big, small), jnp.where(pos, small, big)


def _sigmoid(z):
    return 0.5 * jnp.tanh(0.5 * z) + 0.5


def _silu(z):
    return z * _sigmoid(z)


def _sigmoid_of_twice(zh):
    return 0.5 * jnp.tanh(zh) + 0.5


def _silu_of_twice(zh):
    return zh * (jnp.tanh(zh) + 1.0)


def _log_sigmoid(z):
    return jnp.minimum(z, 0.0) - jnp.log(1.0 + jnp.exp(-jnp.abs(z)))


def _dot(a, b):
    return jnp.dot(a, b, preferred_element_type=F32)


def _split3(a):
    hi = a.astype(BF16)
    r1 = a - hi.astype(F32)
    mid = r1.astype(BF16)
    lo = (r1 - mid.astype(F32)).astype(BF16)
    return hi, mid, lo


def _cumsum_rows(tri, a):
    hi, mid, lo = _split3(a)
    return (_dot(tri, hi) + _dot(tri, mid)) + _dot(tri, lo)


def _rms_rows(y, g):
    return y * lax.rsqrt(jnp.mean(y * y, axis=-1, keepdims=True) + EPS) * g


def _causal_depthwise_conv(buf, w_ref, bias, tile, halo):
    ksize = w_ref.shape[0]
    ext = tile + V7X_SUBLANES
    acc = jnp.broadcast_to(bias, (tile, buf.shape[1]))
    for res in range(min(V7X_SUBLANES, ksize)):
        part = None
        for d in range(res, ksize, V7X_SUBLANES):
            lo = halo - V7X_SUBLANES - (d - res)
            term = w_ref[ksize - 1 - d:ksize - d, :] * buf[lo:lo + ext, :]
            part = term if part is None else part + term
        acc = acc + part[V7X_SUBLANES - res:V7X_SUBLANES - res + tile, :]
    return acc


def _mixer_kernel(layer, tile,
                  x_ref, ng_ref, w_ref, b_ref, mcw_ref, mcb_ref, mng_ref, lbl_ref, hng_ref,
                  cw_ref, cb_ref, lng_ref, lnb_ref, wa_ref, wb_ref, wc_ref, wo_ref,
                  o_ref,
                  qkbuf, ubuf, ct_ref, m_ref, st_ref, ya_ref, yb_ref, a_ref, hq_ref, hk_ref, hb_ref):
    n_chunks = tile // CHUNK

    @pl.when(pl.program_id(1) == 0)
    def _():
        qkbuf[0:QK_HALO, :] = jnp.zeros((QK_HALO, BRANCH_W), F32)
        ubuf[0:U_HALO, :] = jnp.zeros((U_HALO, BRANCH_W), F32)
        ct_ref[...] = jnp.zeros_like(ct_ref)
        m_ref[...] = jnp.zeros_like(m_ref)
        st_ref[...] = jnp.zeros_like(st_ref)

    x = x_ref[...]
    h = _rms_rows(x, ng_ref[...]).astype(BF16)

    def proj(lo, hi):
        return _dot(h, w_ref[:, lo:hi]) + b_ref[:, lo:hi]

    row_i = lax.broadcasted_iota(jnp.int32, (CHUNK, CHUNK), 0)
    col_i = lax.broadcasted_iota(jnp.int32, (CHUNK, CHUNK), 1)
    causal = col_i <= row_i
    tri = causal.astype(BF16)

    qkbuf[QK_HALO:QK_HALO + tile, :] = proj(C_QK, C_QK + BRANCH_W)
    qk = _silu_of_twice(_causal_depthwise_conv(qkbuf, 0.5 * mcw_ref[...], 0.5 * mcb_ref[...], tile, QK_HALO))
    qkbuf[0:QK_HALO, :] = qkbuf[tile:tile + QK_HALO, :]

    zv = proj(C_V, C_V + BRANCH_W)
    og = _sigmoid_of_twice(proj(C_O, C_O + BRANCH_W))
    li_all = proj(C_GI, C_GI + V7X_LANES)
    lf_all = _log_sigmoid(proj(C_GF, C_GF + V7X_LANES))
    lane_c = lax.broadcasted_iota(jnp.int32, (CHUNK, V7X_LANES), 1)
    row_c = lax.broadcasted_iota(jnp.int32, (CHUNK, V7X_LANES), 0)
    lane_grp = lane_c // V7X_SUBLANES
    lane_head = lane_c % V7X_SUBLANES
    ones_blk = jnp.ones((CHUNK, MLSTM_DV), F32)

    def place3(a, first_grp):
        hi, mid, lo = _split3(a)
        zero = jnp.zeros_like(hi)
        return jnp.where(lane_grp == first_grp, hi,
                         jnp.where(lane_grp == first_grp + 1, mid,
                                   jnp.where(lane_grp == first_grp + 2, lo, zero)))

    ones_lo = (lane_grp < 3).astype(BF16)
    ones_hi = jnp.logical_and(lane_grp >= 3, lane_grp < 6).astype(BF16)

    bcs, cs, mus, m_prevs, mu_lasts = [], [], [], [], []
    m_row = m_ref[0:1, :]
    for c in range(n_chunks):
        r0 = c * CHUNK
        bc = _cumsum_rows(tri, lf_all[r0:r0 + CHUNK, :])
        cc = li_all[r0:r0 + CHUNK, :] - bc
        cm = cc
        sh = 1
        while sh < CHUNK:
            cm = jnp.where(row_c >= sh, jnp.maximum(cm, pltpu.roll(cm, sh, axis=0)), cm)
            sh *= 2
        mu_last = jnp.maximum(m_row, cm[CHUNK - 1:CHUNK, :])
        bcs.append(bc)
        cs.append(cc)
        mus.append(jnp.maximum(m_row, cm))
        m_prevs.append(m_row)
        mu_lasts.append(mu_last)
        m_row = bc[CHUNK - 1:CHUNK, :] + mu_last
    m_ref[0:1, :] = m_row

    ct_st = [ct_ref[hd] for hd in range(N_HEADS)]
    for c in range(n_chunks):
        r0 = c * CHUNK
        bc, cc, mu, m_prev, mu_last = bcs[c], cs[c], mus[c], m_prevs[c], mu_lasts[c]
        inter_all = jnp.exp(m_prev - mu)
        rinv_all = jnp.exp(-(bc + mu))
        ws_all = jnp.exp(cc - mu_last)
        keep_all = jnp.exp(m_prev - mu_last)
        lhs = place3(-mu, 0) + ones_hi
        rhs = ones_lo + place3(cc, 3)
        for hd in range(N_HEADS):
            rhs_h = jnp.where(lane_head == hd, rhs, jnp.zeros_like(rhs))
            w = jnp.where(causal, jnp.exp(pl.dot(lhs, rhs_h, trans_b=True)), 0.0)
            inter = inter_all[:, hd:hd + 1]

            qh = qk[r0:r0 + CHUNK, hd * MLSTM_DQK:(hd + 1) * MLSTM_DQK] * (MLSTM_DQK ** -0.5)
            kh = qk[r0:r0 + CHUNK, 256 + hd * MLSTM_DQK:256 + (hd + 1) * MLSTM_DQK]
            vh = zv[r0:r0 + CHUNK, hd * MLSTM_DV:(hd + 1) * MLSTM_DV]
            v_ext = jnp.concatenate([vh, ones_blk], axis=1).astype(BF16)
            ct = ct_st[hd]

            sc = pl.dot(qh.astype(BF16), kh.astype(BF16), trans_b=True) * w
            nd = _dot(sc.astype(BF16), v_ext) + _dot((qh * inter).astype(BF16), ct.astype(BF16))
            num = nd[:, 0:MLSTM_DV]
            den = nd[:, MLSTM_DV:2 * MLSTM_DV]
            h_out = num / jnp.maximum(jnp.abs(den), rinv_all[:, hd:hd + 1])
            hn = _rms_rows(h_out, mng_ref[:, hd * MLSTM_DV:(hd + 1) * MLSTM_DV])
            ya_ref[r0:r0 + CHUNK, hd * MLSTM_DV:(hd + 1) * MLSTM_DV] = (
                og[r0:r0 + CHUNK, hd * MLSTM_DV:(hd + 1) * MLSTM_DV] * hn)

            wk = (ws_all[:, hd:hd + 1] * kh).astype(BF16)
            ct_st[hd] = keep_all[:, hd:hd + 1] * ct + pl.dot(wk, v_ext, trans_a=True)

    for hd in range(N_HEADS):
        ct_ref[hd] = ct_st[hd]

    lrows = [lbl_ref[j:j + 1, :] for j in range(lbl_ref.shape[0])]
    lmax = functools.reduce(jnp.maximum, lrows)
    lexp = [jnp.exp(r - lmax) for r in lrows]
    lsum = functools.reduce(jnp.add, lexp)
    lb = jnp.zeros_like(lmax)
    for j in range(1, layer + 1):
        lb = lb + lexp[j] / lsum

    sg_pos, sg_neg = _sigmoid_pair(proj(C_HF, C_HF + BRANCH_W))
    logf = jnp.log(jnp.maximum(lb + (1.0 - lb) * sg_pos, F_FLOOR))
    hk = (1.0 - lb) * sg_neg
    hq = proj(C_HQ, C_HQ + BRANCH_W) * (HGRN_DK ** -0.5)
    hv = proj(C_HI, C_HI + BRANCH_W).astype(BF16)
    hgate = _silu_of_twice(proj(C_HG, C_HG + BRANCH_W))

    b_chunks = [_cumsum_rows(tri, logf[c * CHUNK:(c + 1) * CHUNK, :]) for c in range(n_chunks)]
    worst = functools.reduce(jnp.minimum, [b[CHUNK - 1:CHUNK, :] for b in b_chunks])
    unsafe = jnp.min(worst) < -HGRN_SAFE_DECAY

    @pl.when(jnp.logical_not(unsafe))
    def _():
        for c in range(n_chunks):
            r0 = c * CHUNK
            b = b_chunks[c]
            r = 0.5 * b[CHUNK - 1:CHUNK, :]
            q_t = (hq[r0:r0 + CHUNK, :] * jnp.exp(b - r)).astype(BF16)
            k_t = (hk[r0:r0 + CHUNK, :] * jnp.exp(r - b)).astype(BF16)
            for hd in range(N_HEADS):
                hs = slice(hd * HGRN_DK, (hd + 1) * HGRN_DK)
                a = pl.dot(q_t[:, hs], k_t[:, hs], trans_b=True)
                a_ref[c * N_HEADS + hd] = jnp.where(causal, a, 0.0).astype(BF16)

    @pl.when(unsafe)
    def _():
        for c in range(n_chunks):
            r0 = c * CHUNK
            hq_ref[...] = hq[r0:r0 + CHUNK, :]
            hk_ref[...] = hk[r0:r0 + CHUNK, :]
            hb_ref[...] = b_chunks[c]
            for hd in range(N_HEADS):
                hs = slice(hd * HGRN_DK, (hd + 1) * HGRN_DK)

                def body(grp, a, hs=hs):
                    g0 = pl.multiple_of(grp * V7X_SUBLANES, V7X_SUBLANES)
                    k_g = hk_ref[pl.ds(g0, V7X_SUBLANES), hs]
                    b_g = hb_ref[pl.ds(g0, V7X_SUBLANES), hs]
                    for j in range(V7X_SUBLANES):
                        e = hq_ref[:, hs] * k_g[j:j + 1, :] * jnp.exp(
                            jnp.minimum(hb_ref[:, hs] - b_g[j:j + 1, :], 0.0))
                        a = jnp.where(col_i == g0 + j, jnp.sum(e, axis=-1, keepdims=True), a)
                    return a

                a = lax.fori_loop(0, CHUNK // V7X_SUBLANES, body, jnp.zeros((CHUNK, CHUNK), F32))
                a_ref[c * N_HEADS + hd] = jnp.where(causal, a, 0.0).astype(BF16)

    s_st = [st_ref[hd] for hd in range(N_HEADS)]
    for c in range(n_chunks):
        r0 = c * CHUNK
        b = b_chunks[c]
        b_last = b[CHUNK - 1:CHUNK, :]
        q_hat = (hq[r0:r0 + CHUNK, :] * jnp.exp(b)).astype(BF16)
        k_hat = (hk[r0:r0 + CHUNK, :] * jnp.exp(b_last - b)).astype(BF16)
        decay = jnp.exp(b_last)
        for hd in range(N_HEADS):
            hs = slice(hd * HGRN_DK, (hd + 1) * HGRN_DK)
            v_h = hv[r0:r0 + CHUNK, hs]
            st = s_st[hd]
            o = _dot(a_ref[c * N_HEADS + hd], v_h) + pl.dot(q_hat[:, hs], st.astype(BF16), trans_b=True)
            s_st[hd] = st * decay[:, hs] + pl.dot(v_h, k_hat[:, hs], trans_a=True)
            yb_ref[r0:r0 + CHUNK, hs] = _rms_rows(o, hng_ref[:, hs]) * hgate[r0:r0 + CHUNK, hs]
    for hd in range(N_HEADS):
        st_ref[hd] = s_st[hd]

    ubuf[U_HALO:U_HALO + tile, :] = proj(C_CA, C_CA + BRANCH_W) * _sigmoid_of_twice(proj(C_CB, C_CB + BRANCH_W))
    acc = _causal_depthwise_conv(ubuf, cw_ref, cb_ref[...], tile, U_HALO)
    ubuf[0:U_HALO, :] = ubuf[tile:tile + U_HALO, :]
    mu = jnp.mean(acc, axis=-1, keepdims=True)
    cen = acc - mu
    var = jnp.mean(cen * cen, axis=-1, keepdims=True)
    yc = _silu_of_twice(cen * lax.rsqrt(var + EPS) * (0.5 * lng_ref[...]) + 0.5 * lnb_ref[...])

    merged = (jnp.tanh(proj(C_GATE, C_GATE + D_MODEL)) + 1.0) * _dot(ya_ref[...].astype(BF16), wa_ref[...])
    merged = merged + (jnp.tanh(proj(C_GATE + D_MODEL, C_GATE + 2 * D_MODEL)) + 1.0) * _dot(
        yb_ref[...].astype(BF16), wb_ref[...])
    merged = merged + (jnp.tanh(proj(C_GATE + 2 * D_MODEL, C_GATE + 3 * D_MODEL)) + 1.0) * _dot(
        yc.astype(BF16), wc_ref[...])
    o_ref[...] = x + _dot(merged.astype(BF16), wo_ref[...])


def _ffn_kernel(final, x_ref, g_ref, wi_ref, wo_ref, fg_ref, o_ref):
    x = x_ref[...]
    h = _rms_rows(x, g_ref[...]).astype(BF16)
    gate = _dot(h, wi_ref[:, 0:D_FF])
    up = _dot(h, wi_ref[:, D_FF:2 * D_FF])
    y = x + _dot((_silu(gate) * up).astype(BF16), wo_ref[...])
    if final:
        y = _rms_rows(y, fg_ref[...])
    o_ref[...] = y


def _resident(shape, index):
    return pl.BlockSpec(shape, index, pipeline_mode=pl.Buffered(1))


def _mixer_call(layer, x, ng, w_mix, b_mix, mcw, mcb, mng, lbl, hng, cw, cb, lng, lnb, wa, wb, wc, wo):
    bsz, seq, d = x.shape
    tile = MIX_TILE
    assert seq % tile == 0 and tile % CHUNK == 0 and d == D_MODEL
    depth = lbl.shape[0]

    def lay3(shape):
        return _resident((None,) + shape, lambda b, t: (layer, 0, 0))

    in_specs = [
        pl.BlockSpec((None, tile, d), lambda b, t: (b, t, 0)),
        lay3((1, d)),
        lay3((d, D_IN_PACKED)),
        lay3((1, D_IN_PACKED)),
        lay3((MLSTM_CONV_K, BRANCH_W)),
        lay3((1, BRANCH_W)),
        lay3((1, BRANCH_W)),
        _resident((depth, BRANCH_W), lambda b, t: (0, 0)),
        lay3((1, BRANCH_W)),
        lay3((CONV_K, BRANCH_W)),
        lay3((1, BRANCH_W)),
        lay3((1, BRANCH_W)),
        lay3((1, BRANCH_W)),
        lay3((BRANCH_W, d)),
        lay3((BRANCH_W, d)),
        lay3((BRANCH_W, d)),
        lay3((d, d)),
    ]
    scratch = [
        pltpu.VMEM((tile + QK_HALO, BRANCH_W), F32),
        pltpu.VMEM((tile + U_HALO, BRANCH_W), F32),
        pltpu.VMEM((N_HEADS, MLSTM_DQK, 2 * MLSTM_DV), F32),
        pltpu.VMEM((V7X_SUBLANES, V7X_LANES), F32),
        pltpu.VMEM((N_HEADS, HGRN_DK, HGRN_DK), F32),
        pltpu.VMEM((tile, BRANCH_W), F32),
        pltpu.VMEM((tile, BRANCH_W), F32),
        pltpu.VMEM(((tile // CHUNK) * N_HEADS, CHUNK, CHUNK), BF16),
        pltpu.VMEM((CHUNK, BRANCH_W), F32),
        pltpu.VMEM((CHUNK, BRANCH_W), F32),
        pltpu.VMEM((CHUNK, BRANCH_W), F32),
    ]
    return pl.pallas_call(
        functools.partial(_mixer_kernel, layer, tile),
        grid=(bsz, seq // tile),
        in_specs=in_specs,
        out_specs=pl.BlockSpec((None, tile, d), lambda b, t: (b, t, 0)),
        out_shape=jax.ShapeDtypeStruct(x.shape, x.dtype),
        scratch_shapes=scratch,
        compiler_params=pltpu.CompilerParams(
            dimension_semantics=("arbitrary", "arbitrary"),
            vmem_limit_bytes=(V7X_VMEM_BYTES * 3) // 4),
        name=f"mixer_l{layer}",
    )(x, ng, w_mix, b_mix, mcw, mcb, mng, lbl, hng, cw, cb, lng, lnb, wa, wb, wc, wo)


def _ffn_call(layer, final, x2, g, wi, wo, fg):
    n, d = x2.shape
    tile = FFN_TILE
    assert n % tile == 0

    def lay3(shape):
        return _resident((None,) + shape, lambda i: (layer, 0, 0))

    return pl.pallas_call(
        functools.partial(_ffn_kernel, final),
        grid=(n // tile,),
        in_specs=[
            pl.BlockSpec((tile, d), lambda i: (i, 0)),
            lay3((1, d)),
            lay3((d, 2 * D_FF)),
            lay3((D_FF, d)),
            _resident((1, d), lambda i: (0, 0)),
        ],
        out_specs=pl.BlockSpec((tile, d), lambda i: (i, 0)),
        out_shape=jax.ShapeDtypeStruct(x2.shape, x2.dtype),
        compiler_params=pltpu.CompilerParams(
            dimension_semantics=("arbitrary",),
            vmem_limit_bytes=(V7X_VMEM_BYTES * 3) // 4),
        name=f"ffn_l{layer}",
    )(x2, g, wi, wo, fg)


def kernel(x, norm_mix_g, w_in, b_in, mlstm_conv_w, mlstm_conv_b, mlstm_norm_g, hgrn_lb_logits, hgrn_norm_g,
           conv_w, conv_b, conv_ln_g, conv_ln_b, w_branch_a, w_branch_b, w_branch_c, w_out, norm_ffn_g,
           w_ffn_in, w_ffn_out, final_norm_g):
    depth = w_in.shape[0]
    bsz, seq, d = x.shape
    n_small = 2 * N_HEADS
    gate_lo = 3 * BRANCH_W
    n_grp = V7X_LANES // V7X_SUBLANES

    def replicate(g):
        g = jnp.concatenate([g, jnp.zeros(g.shape[:-1] + (V7X_SUBLANES - N_HEADS,), g.dtype)], axis=-1)
        return jnp.tile(g, (1,) * (g.ndim - 1) + (n_grp,))

    def pack_cols(a):
        return jnp.concatenate(
            [a[..., :gate_lo], a[..., gate_lo + n_small:],
             replicate(a[..., gate_lo:gate_lo + N_HEADS]),
             replicate(a[..., gate_lo + N_HEADS:gate_lo + n_small])], axis=-1)

    halved = jnp.zeros((D_IN_PACKED,), bool)
    for lo, width in ((C_O, BRANCH_W), (C_HG, BRANCH_W), (C_CB, BRANCH_W), (C_GATE, 3 * D_MODEL)):
        halved = halved.at[lo:lo + width].set(True)
    col_scale = jnp.where(halved, 0.5, 1.0).astype(F32)
    w_mix = (pack_cols(w_in) * col_scale).astype(BF16)
    b_mix = (pack_cols(b_in) * col_scale)[:, None, :]
    row = lambda a: a[:, None, :]
    wa = (0.5 * w_branch_a).astype(BF16)
    wb = (0.5 * w_branch_b).astype(BF16)
    wc = (0.5 * w_branch_c).astype(BF16)
    wo = w_out.astype(BF16)
    wfi = w_ffn_in.astype(BF16)
    wfo = w_ffn_out.astype(BF16)
    fg = final_norm_g[None, :]

    for layer in range(depth):
        x = _mixer_call(layer, x, row(norm_mix_g), w_mix, b_mix, mlstm_conv_w, row(mlstm_conv_b),
                        row(mlstm_norm_g), hgrn_lb_logits, row(hgrn_norm_g), conv_w, row(conv_b),
                        row(conv_ln_g), row(conv_ln_b), wa, wb, wc, wo)
        x = _ffn_call(layer, layer == depth - 1, x.reshape(bsz * seq, d), row(norm_ffn_g), wfi, wfo,
                      fg).reshape(bsz, seq, d)
    return x
```

```python
import functools

import jax
import jax.numpy as jnp
from jax import lax
from jax.experimental import pallas as pl
from jax.experimental.pallas import tpu as pltpu

F32 = jnp.float32
BF16 = jnp.bfloat16

D_MODEL = 1024
BRANCH_W = D_MODEL // 2
N_HEADS = 4
MLSTM_DV = BRANCH_W // N_HEADS
MLSTM_DQK = MLSTM_DV // 2
MLSTM_CONV_K = 4
HGRN_DK = BRANCH_W // N_HEADS
CONV_K = 31
D_FF = 2816
EPS = 1e-6
F_FLOOR = 1e-30

V7X_SUBLANES = 8
V7X_LANES = 128
V7X_VMEM_BYTES = 64 * 1024 * 1024

CHUNK = 128
MIX_TILE = 512
FFN_TILE = 512
QK_HALO = V7X_SUBLANES
U_HALO = 4 * V7X_SUBLANES
HGRN_SAFE_DECAY = 150.0

C_QK = 0
C_V = 512
C_O = 1024
C_HF = 1536
C_HQ = 2048
C_HI = 2560
C_HG = 3072
C_CA = 3584
C_CB = 4096
C_GATE = 4608
C_GI = 7680
C_GF = C_GI + V7X_LANES
D_IN_PACKED = C_GF + V7X_LANES


def _sigmoid_pair(z):
    e = jnp.exp(-jnp.abs(z))
    big = 1.0 / (1.0 + e)
    small = e / (1.0 + e)
    pos = z >= 0
    return jnp.where(pos, big, small), jnp.where(pos, small, big)


def _sigmoid(z):
    return 0.5 * jnp.tanh(0.5 * z) + 0.5


def _silu(z):
    return z * _sigmoid(z)


def _sigmoid_of_twice(zh):
    return 0.5 * jnp.tanh(zh) + 0.5


def _silu_of_twice(zh):
    return zh * (jnp.tanh(zh) + 1.0)


def _log_sigmoid(z):
    return jnp.minimum(z, 0.0) - jnp.log(1.0 + jnp.exp(-jnp.abs(z)))


def _dot(a, b):
    return jnp.dot(a, b, preferred_element_type=F32)


def _split3(a):
    hi = a.astype(BF16)
    r1 = a - hi.astype(F32)
    mid = r1.astype(BF16)
    lo = (r1 - mid.astype(F32)).astype(BF16)
    return hi, mid, lo


def _cumsum_rows(tri, a):
    hi, mid, lo = _split3(a)
    return (_dot(tri, hi) + _dot(tri, mid)) + _dot(tri, lo)


def _rms_rows(y, g):
    return y * lax.rsqrt(jnp.mean(y * y, axis=-1, keepdims=True) + EPS) * g


def _causal_depthwise_conv(buf, w_ref, bias, tile, halo, anchors=()):
    ksize = w_ref.shape[0]
    ext = tile + V7X_SUBLANES
    acc = jnp.broadcast_to(bias, (tile, buf.shape[1]))
    for res in range(min(V7X_SUBLANES, ksize)):
        part = None
        for d in range(res, ksize, V7X_SUBLANES):
            lo = halo - V7X_SUBLANES - (d - res)
            w_row = w_ref[ksize - 1 - d:ksize - d, :]
            if part is None and res < len(anchors):
                w_row = w_row + anchors[res]
            term = w_row * buf[lo:lo + ext, :]
            part = term if part is None else part + term
        acc = acc + part[V7X_SUBLANES - res:V7X_SUBLANES - res + tile, :]
    return acc


def _mixer_kernel(layer, tile,
                  x_ref, ng_ref, w_ref, b_ref, mcw_ref, mcb_ref, mng_ref, lbl_ref, hng_ref,
                  cw_ref, cb_ref, lng_ref, lnb_ref, wa_ref, wb_ref, wc_ref, wo_ref,
                  o_ref,
                  qkbuf, ubuf, ct_ref, m_ref, st_ref, ya_ref, yb_ref, a_ref, hq_ref, hk_ref, hb_ref):
    n_chunks = tile // CHUNK

    @pl.when(pl.program_id(1) == 0)
    def _():
        qkbuf[0:QK_HALO, :] = jnp.zeros((QK_HALO, BRANCH_W), F32)
        ubuf[0:U_HALO, :] = jnp.zeros((U_HALO, BRANCH_W), F32)
        ct_ref[...] = jnp.zeros_like(ct_ref)
        m_ref[...] = jnp.zeros_like(m_ref)
        st_ref[...] = jnp.zeros_like(st_ref)

    x = x_ref[...]
    h = _rms_rows(x, ng_ref[...]).astype(BF16)

    def proj(lo, hi):
        return _dot(h, w_ref[:, lo:hi]) + b_ref[:, lo:hi]

    row_i = lax.broadcasted_iota(jnp.int32, (CHUNK, CHUNK), 0)
    col_i = lax.broadcasted_iota(jnp.int32, (CHUNK, CHUNK), 1)
    causal = col_i <= row_i
    tri = causal.astype(BF16)

    qkbuf[QK_HALO:QK_HALO + tile, :] = proj(C_QK, C_QK + BRANCH_W)
    qk = _silu_of_twice(_causal_depthwise_conv(qkbuf, 0.5 * mcw_ref[...], 0.5 * mcb_ref[...], tile, QK_HALO))
    qkbuf[0:QK_HALO, :] = qkbuf[tile:tile + QK_HALO, :]

    zv = proj(C_V, C_V + BRANCH_W)
    og = _sigmoid_of_twice(proj(C_O, C_O + BRANCH_W))
    li_all = proj(C_GI, C_GI + V7X_LANES)
    lf_all = _log_sigmoid(proj(C_GF, C_GF + V7X_LANES))
    lane_c = lax.broadcasted_iota(jnp.int32, (CHUNK, V7X_LANES), 1)
    row_c = lax.broadcasted_iota(jnp.int32, (CHUNK, V7X_LANES), 0)
    lane_grp = lane_c // V7X_SUBLANES
    lane_head = lane_c % V7X_SUBLANES
    ones_blk = jnp.ones((CHUNK, MLSTM_DV), F32)

    def place3(a, first_grp):
        hi, mid, lo = _split3(a)
        zero = jnp.zeros_like(hi)
        return jnp.where(lane_grp == first_grp, hi,
                         jnp.where(lane_grp == first_grp + 1, mid,
                                   jnp.where(lane_grp == first_grp + 2, lo, zero)))

    ones_lo = (lane_grp < 3).astype(BF16)
    ones_hi = jnp.logical_and(lane_grp >= 3, lane_grp < 6).astype(BF16)

    bcs, cs, mus, m_prevs, mu_lasts = [], [], [], [], []
    m_row = m_ref[0:1, :]
    for c in range(n_chunks):
        r0 = c * CHUNK
        bc = _cumsum_rows(tri, lf_all[r0:r0 + CHUNK, :])
        cc = li_all[r0:r0 + CHUNK, :] - bc
        cm = cc
        sh = 1
        while sh < CHUNK:
            cm = jnp.where(row_c >= sh, jnp.maximum(cm, pltpu.roll(cm, sh, axis=0)), cm)
            sh *= 2
        mu_last = jnp.maximum(m_row, cm[CHUNK - 1:CHUNK, :])
        bcs.append(bc)
        cs.append(cc)
        mus.append(jnp.maximum(m_row, cm))
        m_prevs.append(m_row)
        mu_lasts.append(mu_last)
        m_row = bc[CHUNK - 1:CHUNK, :] + mu_last
    m_ref[0:1, :] = m_row

    ct_st = [ct_ref[hd] for hd in range(N_HEADS)]
    for c in range(n_chunks):
        r0 = c * CHUNK
        bc, cc, mu, m_prev, mu_last = bcs[c], cs[c], mus[c], m_prevs[c], mu_lasts[c]
        inter_all = jnp.exp(m_prev - mu)
        rinv_all = jnp.exp(-(bc + mu))
        ws_all = jnp.exp(cc - mu_last)
        keep_all = jnp.exp(m_prev - mu_last)
        lhs = place3(-mu, 0) + ones_hi
        rhs = ones_lo + place3(cc, 3)
        for hd in range(N_HEADS):
            rhs_h = jnp.where(lane_head == hd, rhs, jnp.zeros_like(rhs))
            w = jnp.where(causal, jnp.exp(pl.dot(lhs, rhs_h, trans_b=True)), 0.0)
            inter = inter_all[:, hd:hd + 1]

            qh = qk[r0:r0 + CHUNK, hd * MLSTM_DQK:(hd + 1) * MLSTM_DQK] * (MLSTM_DQK ** -0.5)
            kh = qk[r0:r0 + CHUNK, 256 + hd * MLSTM_DQK:256 + (hd + 1) * MLSTM_DQK]
            vh = zv[r0:r0 + CHUNK, hd * MLSTM_DV:(hd + 1) * MLSTM_DV]
            v_ext = jnp.concatenate([vh, ones_blk], axis=1).astype(BF16)
            ct = ct_st[hd]

            sc = pl.dot(qh.astype(BF16), kh.astype(BF16), trans_b=True) * w
            nd = _dot(sc.astype(BF16), v_ext) + _dot((qh * inter).astype(BF16), ct.astype(BF16))
            num = nd[:, 0:MLSTM_DV]
            den = nd[:, MLSTM_DV:2 * MLSTM_DV]
            h_out = num / jnp.maximum(jnp.abs(den), rinv_all[:, hd:hd + 1])
            hn = _rms_rows(h_out, mng_ref[:, hd * MLSTM_DV:(hd + 1) * MLSTM_DV])
            ya_ref[r0:r0 + CHUNK, hd * MLSTM_DV:(hd + 1) * MLSTM_DV] = (
                og[r0:r0 + CHUNK, hd * MLSTM_DV:(hd + 1) * MLSTM_DV] * hn)

            wk = (ws_all[:, hd:hd + 1] * kh).astype(BF16)
            ct_st[hd] = keep_all[:, hd:hd + 1] * ct + pl.dot(wk, v_ext, trans_a=True)

    for hd in range(N_HEADS):
        ct_ref[hd] = ct_st[hd]

    lrows = [lbl_ref[j:j + 1, :] for j in range(lbl_ref.shape[0])]
    lmax = functools.reduce(jnp.maximum, lrows)
    lexp = [jnp.exp(r - lmax) for r in lrows]
    lsum = functools.reduce(jnp.add, lexp)
    lb = jnp.zeros_like(lmax)
    for j in range(1, layer + 1):
        lb = lb + lexp[j] / lsum

    sg_pos, sg_neg = _sigmoid_pair(proj(C_HF, C_HF + BRANCH_W))
    logf = jnp.log(jnp.maximum(lb + (1.0 - lb) * sg_pos, F_FLOOR))
    hk = (1.0 - lb) * sg_neg
    hq = proj(C_HQ, C_HQ + BRANCH_W) * (HGRN_DK ** -0.5)
    hv = proj(C_HI, C_HI + BRANCH_W).astype(BF16)
    hgate = _silu_of_twice(proj(C_HG, C_HG + BRANCH_W))

    b_chunks = [_cumsum_rows(tri, logf[c * CHUNK:(c + 1) * CHUNK, :]) for c in range(n_chunks)]
    worst = functools.reduce(jnp.minimum, [b[CHUNK - 1:CHUNK, :] for b in b_chunks])
    unsafe = jnp.min(worst) < -HGRN_SAFE_DECAY

    @pl.when(jnp.logical_not(unsafe))
    def _():
        for c in range(n_chunks):
            r0 = c * CHUNK
            b = b_chunks[c]
            r = 0.5 * b[CHUNK - 1:CHUNK, :]
            q_t = (hq[r0:r0 + CHUNK, :] * jnp.exp(b - r)).astype(BF16)
            k_t = (hk[r0:r0 + CHUNK, :] * jnp.exp(r - b)).astype(BF16)
            for hd in range(N_HEADS):
                hs = slice(hd * HGRN_DK, (hd + 1) * HGRN_DK)
                a = pl.dot(q_t[:, hs], k_t[:, hs], trans_b=True)
                a_ref[c * N_HEADS + hd] = jnp.where(causal, a, 0.0).astype(BF16)

    @pl.when(unsafe)
    def _():
        for c in range(n_chunks):
            r0 = c * CHUNK
            hq_ref[...] = hq[r0:r0 + CHUNK, :]
            hk_ref[...] = hk[r0:r0 + CHUNK, :]
            hb_ref[...] = b_chunks[c]
            for hd in range(N_HEADS):
                hs = slice(hd * HGRN_DK, (hd + 1) * HGRN_DK)

                def body(grp, a, hs=hs):
                    g0 = pl.multiple_of(grp * V7X_SUBLANES, V7X_SUBLANES)
                    k_g = hk_ref[pl.ds(g0, V7X_SUBLANES), hs]
                    b_g = hb_ref[pl.ds(g0, V7X_SUBLANES), hs]
                    for j in range(V7X_SUBLANES):
                        e = hq_ref[:, hs] * k_g[j:j + 1, :] * jnp.exp(
                            jnp.minimum(hb_ref[:, hs] - b_g[j:j + 1, :], 0.0))
                        a = jnp.where(col_i == g0 + j, jnp.sum(e, axis=-1, keepdims=True), a)
                    return a

                a = lax.fori_loop(0, CHUNK // V7X_SUBLANES, body, jnp.zeros((CHUNK, CHUNK), F32))
                a_ref[c * N_HEADS + hd] = jnp.where(causal, a, 0.0).astype(BF16)

    s_st = [st_ref[hd] for hd in range(N_HEADS)]
    for c in range(n_chunks):
        r0 = c * CHUNK
        b = b_chunks[c]
        b_last = b[CHUNK - 1:CHUNK, :]
        q_hat = (hq[r0:r0 + CHUNK, :] * jnp.exp(b)).astype(BF16)
        k_hat = (hk[r0:r0 + CHUNK, :] * jnp.exp(b_last - b)).astype(BF16)
        decay = jnp.exp(b_last)
        for hd in range(N_HEADS):
            hs = slice(hd * HGRN_DK, (hd + 1) * HGRN_DK)
            v_h = hv[r0:r0 + CHUNK, hs]
            st = s_st[hd]
            o = _dot(a_ref[c * N_HEADS + hd], v_h) + pl.dot(q_hat[:, hs], st.astype(BF16), trans_b=True)
            s_st[hd] = st * decay[:, hs] + pl.dot(v_h, k_hat[:, hs], trans_a=True)
            yb_ref[r0:r0 + CHUNK, hs] = _rms_rows(o, hng_ref[:, hs]) * hgate[r0:r0 + CHUNK, hs]
    for hd in range(N_HEADS):
        st_ref[hd] = s_st[hd]

    ubuf[U_HALO:U_HALO + tile, :] = proj(C_CA, C_CA + BRANCH_W) * _sigmoid_of_twice(proj(C_CB, C_CB + BRANCH_W))
    gate_w = BRANCH_W
    gates, anchors = [], []
    for j in range((3 * D_MODEL) // gate_w):
        g = jnp.tanh(proj(C_GATE + j * gate_w, C_GATE + (j + 1) * gate_w)) + 1.0
        gates.append(g)
        anchors.append(jnp.where(jnp.maximum(g[0:1, :], g[tile - 1:tile, :]) > 3.0, 1.0, 0.0))
    acc = _causal_depthwise_conv(ubuf, cw_ref, cb_ref[...], tile, U_HALO, anchors)
    ubuf[0:U_HALO, :] = ubuf[tile:tile + U_HALO, :]
    mu = jnp.mean(acc, axis=-1, keepdims=True)
    cen = acc - mu
    var = jnp.mean(cen * cen, axis=-1, keepdims=True)
    yc = _silu_of_twice(cen * lax.rsqrt(var + EPS) * (0.5 * lng_ref[...]) + 0.5 * lnb_ref[...])

    per_branch = D_MODEL // gate_w
    g_a, g_b, g_c = (jnp.concatenate(gates[i * per_branch:(i + 1) * per_branch], axis=1) for i in range(3))
    merged = g_a * _dot(ya_ref[...].astype(BF16), wa_ref[...])
    merged = merged + g_b * _dot(yb_ref[...].astype(BF16), wb_ref[...])
    merged = merged + g_c * _dot(yc.astype(BF16), wc_ref[...])
    o_ref[...] = x + _dot(merged.astype(BF16), wo_ref[...])


def _ffn_kernel(final, x_ref, g_ref, wi_ref, wo_ref, fg_ref, o_ref):
    x = x_ref[...]
    h = _rms_rows(x, g_ref[...]).astype(BF16)
    gate = _dot(h, wi_ref[:, 0:D_FF])
    up = _dot(h, wi_ref[:, D_FF:2 * D_FF])
    y = x + _dot((_silu(gate) * up).astype(BF16), wo_ref[...])
    if final:
        y = _rms_rows(y, fg_ref[...])
    o_ref[...] = y


def _resident(shape, index):
    return pl.BlockSpec(shape, index, pipeline_mode=pl.Buffered(1))


def _mixer_call(layer, x, ng, w_mix, b_mix, mcw, mcb, mng, lbl, hng, cw, cb, lng, lnb, wa, wb, wc, wo):
    bsz, seq, d = x.shape
    tile = MIX_TILE
    assert seq % tile == 0 and tile % CHUNK == 0 and d == D_MODEL
    depth = lbl.shape[0]

    def lay3(shape):
        return _resident((None,) + shape, lambda b, t: (layer, 0, 0))

    in_specs = [
        pl.BlockSpec((None, tile, d), lambda b, t: (b, t, 0)),
        lay3((1, d)),
        lay3((d, D_IN_PACKED)),
        lay3((1, D_IN_PACKED)),
        lay3((MLSTM_CONV_K, BRANCH_W)),
        lay3((1, BRANCH_W)),
        lay3((1, BRANCH_W)),
        _resident((depth, BRANCH_W), lambda b, t: (0, 0)),
        lay3((1, BRANCH_W)),
        lay3((CONV_K, BRANCH_W)),
        lay3((1, BRANCH_W)),
        lay3((1, BRANCH_W)),
        lay3((1, BRANCH_W)),
        lay3((BRANCH_W, d)),
        lay3((BRANCH_W, d)),
        lay3((BRANCH_W, d)),
        lay3((d, d)),
    ]
    scratch = [
        pltpu.VMEM((tile + QK_HALO, BRANCH_W), F32),
        pltpu.VMEM((tile + U_HALO, BRANCH_W), F32),
        pltpu.VMEM((N_HEADS, MLSTM_DQK, 2 * MLSTM_DV), F32),
        pltpu.VMEM((V7X_SUBLANES, V7X_LANES), F32),
        pltpu.VMEM((N_HEADS, HGRN_DK, HGRN_DK), F32),
        pltpu.VMEM((tile, BRANCH_W), F32),
        pltpu.VMEM((tile, BRANCH_W), F32),
        pltpu.VMEM(((tile // CHUNK) * N_HEADS, CHUNK, CHUNK), BF16),
        pltpu.VMEM((CHUNK, BRANCH_W), F32),
        pltpu.VMEM((CHUNK, BRANCH_W), F32),
        pltpu.VMEM((CHUNK, BRANCH_W), F32),
    ]
    return pl.pallas_call(
        functools.partial(_mixer_kernel, layer, tile),
        grid=(bsz, seq // tile),
        in_specs=in_specs,
        out_specs=pl.BlockSpec((None, tile, d), lambda b, t: (b, t, 0)),
        out_shape=jax.ShapeDtypeStruct(x.shape, x.dtype),
        scratch_shapes=scratch,
        compiler_params=pltpu.CompilerParams(
            dimension_semantics=("arbitrary", "arbitrary"),
            vmem_limit_bytes=(V7X_VMEM_BYTES * 7) // 8),
        name=f"mixer_l{layer}",
    )(x, ng, w_mix, b_mix, mcw, mcb, mng, lbl, hng, cw, cb, lng, lnb, wa, wb, wc, wo)


def _ffn_call(layer, final, x2, g, wi, wo, fg):
    n, d = x2.shape
    tile = FFN_TILE
    assert n % tile == 0

    def lay3(shape):
        return _resident((None,) + shape, lambda i: (layer, 0, 0))

    return pl.pallas_call(
        functools.partial(_ffn_kernel, final),
        grid=(n // tile,),
        in_specs=[
            pl.BlockSpec((tile, d), lambda i: (i, 0)),
            lay3((1, d)),
            lay3((d, 2 * D_FF)),
            lay3((D_FF, d)),
            _resident((1, d), lambda i: (0, 0)),
        ],
        out_specs=pl.BlockSpec((tile, d), lambda i: (i, 0)),
        out_shape=jax.ShapeDtypeStruct(x2.shape, x2.dtype),
        compiler_params=pltpu.CompilerParams(
            dimension_semantics=("arbitrary",),
            vmem_limit_bytes=(V7X_VMEM_BYTES * 3) // 4),
        name=f"ffn_l{layer}",
    )(x2, g, wi, wo, fg)


def kernel(x, norm_mix_g, w_in, b_in, mlstm_conv_w, mlstm_conv_b, mlstm_norm_g, hgrn_lb_logits, hgrn_norm_g,
           conv_w, conv_b, conv_ln_g, conv_ln_b, w_branch_a, w_branch_b, w_branch_c, w_out, norm_ffn_g,
           w_ffn_in, w_ffn_out, final_norm_g):
    depth = w_in.shape[0]
    bsz, seq, d = x.shape
    n_small = 2 * N_HEADS
    gate_lo = 3 * BRANCH_W
    n_grp = V7X_LANES // V7X_SUBLANES

    def replicate(g):
        g = jnp.concatenate([g, jnp.zeros(g.shape[:-1] + (V7X_SUBLANES - N_HEADS,), g.dtype)], axis=-1)
        return jnp.tile(g, (1,) * (g.ndim - 1) + (n_grp,))

    def pack_cols(a):
        return jnp.concatenate(
            [a[..., :gate_lo], a[..., gate_lo + n_small:],
             replicate(a[..., gate_lo:gate_lo + N_HEADS]),
             replicate(a[..., gate_lo + N_HEADS:gate_lo + n_small])], axis=-1)

    halved = jnp.zeros((D_IN_PACKED,), bool)
    for lo, width in ((C_O, BRANCH_W), (C_HG, BRANCH_W), (C_CB, BRANCH_W), (C_GATE, 3 * D_MODEL)):
        halved = halved.at[lo:lo + width].set(True)
    col_scale = jnp.where(halved, 0.5, 1.0).astype(F32)
    w_mix = pack_cols(w_in.astype(BF16)) * col_scale.astype(BF16)
    b_mix = (pack_cols(b_in) * col_scale)[:, None, :]
    row = lambda a: a[:, None, :]
    wa = (0.5 * w_branch_a).astype(BF16)
    wb = (0.5 * w_branch_b).astype(BF16)
    wc = (0.5 * w_branch_c).astype(BF16)
    wo = w_out.astype(BF16)
    wfi = w_ffn_in.astype(BF16)
    wfo = w_ffn_out.astype(BF16)
    fg = final_norm_g[None, :]

    for layer in range(depth):
        x = _mixer_call(layer, x, row(norm_mix_g), w_mix, b_mix, mlstm_conv_w, row(mlstm_conv_b),
                        row(mlstm_norm_g), hgrn_lb_logits, row(hgrn_norm_g), conv_w, row(conv_b),
                        row(conv_ln_g), row(conv_ln_b), wa, wb, wc, wo)
        x = _ffn_call(layer, layer == depth - 1, x.reshape(bsz * seq, d), row(norm_ffn_g), wfi, wfo,
                      fg).reshape(bsz, seq, d)
    return x
```

```python
import functools

import jax
import jax.numpy as jnp
from jax import lax
from jax.experimental import pallas as pl
from jax.experimental.pallas import tpu as pltpu

F32 = jnp.float32
BF16 = jnp.bfloat16

D_MODEL = 1024
BRANCH_W = D_MODEL // 2
N_HEADS = 4
MLSTM_DV = BRANCH_W // N_HEADS
MLSTM_DQK = MLSTM_DV // 2
MLSTM_CONV_K = 4
HGRN_DK = BRANCH_W // N_HEADS
CONV_K = 31
D_FF = 2816
EPS = 1e-6
F_FLOOR = 1e-30

V7X_SUBLANES = 8
V7X_LANES = 128
V7X_VMEM_BYTES = 64 * 1024 * 1024

CHUNK = 128
MLSTM_CHUNK = 256
MIX_TILE = 512
FFN_TILE = 512
QK_HALO = V7X_SUBLANES
U_HALO = 4 * V7X_SUBLANES
HGRN_SAFE_DECAY = 150.0

C_QK = 0
C_V = 512
C_O = 1024
C_HF = 1536
C_HQ = 2048
C_HI = 2560
C_HG = 3072
C_CA = 3584
C_CB = 4096
C_GATE = 4608
C_GI = 7680
C_GF = C_GI + V7X_LANES
D_IN_PACKED = C_GF + V7X_LANES


def _sigmoid_pair(z):
    e = jnp.exp(-jnp.abs(z))
    big = 1.0 / (1.0 + e)
    small = e / (1.0 + e)
    pos = z >= 0
    return jnp.where(pos, big, small), jnp.where(pos, small, big)


def _sigmoid(z):
    return 0.5 * jnp.tanh(0.5 * z) + 0.5


def _silu(z):
    return z * _sigmoid(z)


def _sigmoid_of_twice(zh):
    return 0.5 * jnp.tanh(zh) + 0.5


def _silu_of_twice(zh):
    return zh * (jnp.tanh(zh) + 1.0)


def _log_sigmoid(z):
    return jnp.minimum(z, 0.0) - jnp.log(1.0 + jnp.exp(-jnp.abs(z)))


def _dot(a, b):
    return jnp.dot(a, b, preferred_element_type=F32)


def _split3(a):
    hi = a.astype(BF16)
    r1 = a - hi.astype(F32)
    mid = r1.astype(BF16)
    lo = (r1 - mid.astype(F32)).astype(BF16)
    return hi, mid, lo


def _cumsum_rows(tri, a):
    hi, mid, lo = _split3(a)
    return (_dot(tri, hi) + _dot(tri, mid)) + _dot(tri, lo)


def _rms_rows(y, g):
    return y * lax.rsqrt(jnp.mean(y * y, axis=-1, keepdims=True) + EPS) * g


def _causal_depthwise_conv(buf, w_ref, bias, tile, halo, anchors=()):
    ksize = w_ref.shape[0]
    ext = tile + V7X_SUBLANES
    acc = jnp.broadcast_to(bias, (tile, buf.shape[1]))
    for res in range(min(V7X_SUBLANES, ksize)):
        part = None
        for d in range(res, ksize, V7X_SUBLANES):
            lo = halo - V7X_SUBLANES - (d - res)
            w_row = w_ref[ksize - 1 - d:ksize - d, :]
            if part is None and res < len(anchors):
                w_row = w_row + anchors[res]
            term = w_row * buf[lo:lo + ext, :]
            part = term if part is None else part + term
        acc = acc + part[V7X_SUBLANES - res:V7X_SUBLANES - res + tile, :]
    return acc


def _mixer_kernel(layer, tile,
                  x_ref, ng_ref, w_ref, b_ref, mcw_ref, mcb_ref, mng_ref, lbl_ref, hng_ref,
                  cw_ref, cb_ref, lng_ref, lnb_ref, wa_ref, wb_ref, wc_ref, wo_ref,
                  o_ref,
                  qkbuf, ubuf, ct_ref, m_ref, st_ref, ya_ref, yb_ref, a_ref, hq_ref, hk_ref, hb_ref):
    n_chunks = tile // CHUNK

    @pl.when(pl.program_id(1) == 0)
    def _():
        qkbuf[0:QK_HALO, :] = jnp.zeros((QK_HALO, BRANCH_W), F32)
        ubuf[0:U_HALO, :] = jnp.zeros((U_HALO, BRANCH_W), F32)
        ct_ref[...] = jnp.zeros_like(ct_ref)
        m_ref[...] = jnp.zeros_like(m_ref)
        st_ref[...] = jnp.zeros_like(st_ref)

    x = x_ref[...]
    h = _rms_rows(x, ng_ref[...]).astype(BF16)

    def proj(lo, hi):
        return _dot(h, w_ref[:, lo:hi]) + b_ref[:, lo:hi]

    row_i = lax.broadcasted_iota(jnp.int32, (CHUNK, CHUNK), 0)
    col_i = lax.broadcasted_iota(jnp.int32, (CHUNK, CHUNK), 1)
    causal = col_i <= row_i
    tri = causal.astype(BF16)

    n_mchunks = tile // MLSTM_CHUNK
    mrow_i = lax.broadcasted_iota(jnp.int32, (MLSTM_CHUNK, MLSTM_CHUNK), 0)
    mcol_i = lax.broadcasted_iota(jnp.int32, (MLSTM_CHUNK, MLSTM_CHUNK), 1)
    mcausal = mcol_i <= mrow_i
    mtri = mcausal.astype(BF16)
    qkbuf[QK_HALO:QK_HALO + tile, :] = proj(C_QK, C_QK + BRANCH_W)
    qk = _silu_of_twice(_causal_depthwise_conv(qkbuf, 0.5 * mcw_ref[...], 0.5 * mcb_ref[...], tile, QK_HALO))
    qkbuf[0:QK_HALO, :] = qkbuf[tile:tile + QK_HALO, :]

    zv = proj(C_V, C_V + BRANCH_W)
    og = _sigmoid_of_twice(proj(C_O, C_O + BRANCH_W))
    li_all = proj(C_GI, C_GI + V7X_LANES)
    lf_all = _log_sigmoid(proj(C_GF, C_GF + V7X_LANES))
    lane_c = lax.broadcasted_iota(jnp.int32, (MLSTM_CHUNK, V7X_LANES), 1)
    row_c = lax.broadcasted_iota(jnp.int32, (MLSTM_CHUNK, V7X_LANES), 0)
    lane_grp = lane_c // V7X_SUBLANES
    lane_head = lane_c % V7X_SUBLANES
    ones_blk = jnp.ones((MLSTM_CHUNK, MLSTM_DV), F32)

    def place3(a, first_grp):
        hi, mid, lo = _split3(a)
        zero = jnp.zeros_like(hi)
        return jnp.where(lane_grp == first_grp, hi,
                         jnp.where(lane_grp == first_grp + 1, mid,
                                   jnp.where(lane_grp == first_grp + 2, lo, zero)))

    ones_lo = (lane_grp < 3).astype(BF16)
    ones_hi = jnp.logical_and(lane_grp >= 3, lane_grp < 6).astype(BF16)

    bcs, cs, mus, m_prevs, mu_lasts = [], [], [], [], []
    m_row = m_ref[0:1, :]
    for c in range(n_mchunks):
        r0 = c * MLSTM_CHUNK
        bc = _cumsum_rows(mtri, lf_all[r0:r0 + MLSTM_CHUNK, :])
        cc = li_all[r0:r0 + MLSTM_CHUNK, :] - bc
        cm = cc
        sh = 1
        while sh < MLSTM_CHUNK:
            cm = jnp.where(row_c >= sh, jnp.maximum(cm, pltpu.roll(cm, sh, axis=0)), cm)
            sh *= 2
        mu_last = jnp.maximum(m_row, cm[MLSTM_CHUNK - 1:MLSTM_CHUNK, :])
        bcs.append(bc)
        cs.append(cc)
        mus.append(jnp.maximum(m_row, cm))
        m_prevs.append(m_row)
        mu_lasts.append(mu_last)
        m_row = bc[MLSTM_CHUNK - 1:MLSTM_CHUNK, :] + mu_last
    m_ref[0:1, :] = m_row

    ct_st = [ct_ref[hd] for hd in range(N_HEADS)]
    for c in range(n_mchunks):
        r0 = c * MLSTM_CHUNK
        bc, cc, mu, m_prev, mu_last = bcs[c], cs[c], mus[c], m_prevs[c], mu_lasts[c]
        inter_all = jnp.exp(m_prev - mu)
        rinv_all = jnp.exp(-(bc + mu))
        ws_all = jnp.exp(cc - mu_last)
        keep_all = jnp.exp(m_prev - mu_last)
        lhs = place3(-mu, 0) + ones_hi
        rhs = ones_lo + place3(cc, 3)
        for hd in range(N_HEADS):
            rhs_h = jnp.where(lane_head == hd, rhs, jnp.zeros_like(rhs))
            w = jnp.where(mcausal, jnp.exp(pl.dot(lhs, rhs_h, trans_b=True)), 0.0)
            inter = inter_all[:, hd:hd + 1]

            qh = qk[r0:r0 + MLSTM_CHUNK, hd * MLSTM_DQK:(hd + 1) * MLSTM_DQK] * (MLSTM_DQK ** -0.5)
            kh = qk[r0:r0 + MLSTM_CHUNK, 256 + hd * MLSTM_DQK:256 + (hd + 1) * MLSTM_DQK]
            vh = zv[r0:r0 + MLSTM_CHUNK, hd * MLSTM_DV:(hd + 1) * MLSTM_DV]
            v_ext = jnp.concatenate([vh, ones_blk], axis=1).astype(BF16)
            ct = ct_st[hd]

            sc = pl.dot(qh.astype(BF16), kh.astype(BF16), trans_b=True) * w
            nd = _dot(sc.astype(BF16), v_ext) + _dot((qh * inter).astype(BF16), ct.astype(BF16))
            num = nd[:, 0:MLSTM_DV]
            den = nd[:, MLSTM_DV:2 * MLSTM_DV]
            h_out = num / jnp.maximum(jnp.abs(den), rinv_all[:, hd:hd + 1])
            hn = _rms_rows(h_out, mng_ref[:, hd * MLSTM_DV:(hd + 1) * MLSTM_DV])
            ya_ref[r0:r0 + MLSTM_CHUNK, hd * MLSTM_DV:(hd + 1) * MLSTM_DV] = (
                og[r0:r0 + MLSTM_CHUNK, hd * MLSTM_DV:(hd + 1) * MLSTM_DV] * hn)

            wk = (ws_all[:, hd:hd + 1] * kh).astype(BF16)
            ct_st[hd] = keep_all[:, hd:hd + 1] * ct + pl.dot(wk, v_ext, trans_a=True)

    for hd in range(N_HEADS):
        ct_ref[hd] = ct_st[hd]

    lrows = [lbl_ref[j:j + 1, :] for j in range(lbl_ref.shape[0])]
    lmax = functools.reduce(jnp.maximum, lrows)
    lexp = [jnp.exp(r - lmax) for r in lrows]
    lsum = functools.reduce(jnp.add, lexp)
    lb = jnp.zeros_like(lmax)
    for j in range(1, layer + 1):
        lb = lb + lexp[j] / lsum

    sg_pos, sg_neg = _sigmoid_pair(proj(C_HF, C_HF + BRANCH_W))
    logf = jnp.log(jnp.maximum(lb + (1.0 - lb) * sg_pos, F_FLOOR))
    hk = (1.0 - lb) * sg_neg
    hq = proj(C_HQ, C_HQ + BRANCH_W) * (HGRN_DK ** -0.5)
    hv = proj(C_HI, C_HI + BRANCH_W).astype(BF16)
    hgate = _silu_of_twice(proj(C_HG, C_HG + BRANCH_W))

    b_chunks = [_cumsum_rows(tri, logf[c * CHUNK:(c + 1) * CHUNK, :]) for c in range(n_chunks)]
    worst = functools.reduce(jnp.minimum, [b[CHUNK - 1:CHUNK, :] for b in b_chunks])
    unsafe = jnp.min(worst) < -HGRN_SAFE_DECAY

    @pl.when(jnp.logical_not(unsafe))
    def _():
        for c in range(n_chunks):
            r0 = c * CHUNK
            b = b_chunks[c]
            r = 0.5 * b[CHUNK - 1:CHUNK, :]
            q_t = (hq[r0:r0 + CHUNK, :] * jnp.exp(b - r)).astype(BF16)
            k_t = (hk[r0:r0 + CHUNK, :] * jnp.exp(r - b)).astype(BF16)
            for hd in range(N_HEADS):
                hs = slice(hd * HGRN_DK, (hd + 1) * HGRN_DK)
                a = pl.dot(q_t[:, hs], k_t[:, hs], trans_b=True)
                a_ref[c * N_HEADS + hd] = jnp.where(causal, a, 0.0).astype(BF16)

    @pl.when(unsafe)
    def _():
        for c in range(n_chunks):
            r0 = c * CHUNK
            hq_ref[...] = hq[r0:r0 + CHUNK, :]
            hk_ref[...] = hk[r0:r0 + CHUNK, :]
            hb_ref[...] = b_chunks[c]
            for hd in range(N_HEADS):
                hs = slice(hd * HGRN_DK, (hd + 1) * HGRN_DK)

                def body(grp, a, hs=hs):
                    g0 = pl.multiple_of(grp * V7X_SUBLANES, V7X_SUBLANES)
                    k_g = hk_ref[pl.ds(g0, V7X_SUBLANES), hs]
                    b_g = hb_ref[pl.ds(g0, V7X_SUBLANES), hs]
                    for j in range(V7X_SUBLANES):
                        e = hq_ref[:, hs] * k_g[j:j + 1, :] * jnp.exp(
                            jnp.minimum(hb_ref[:, hs] - b_g[j:j + 1, :], 0.0))
                        a = jnp.where(col_i == g0 + j, jnp.sum(e, axis=-1, keepdims=True), a)
                    return a

                a = lax.fori_loop(0, CHUNK // V7X_SUBLANES, body, jnp.zeros((CHUNK, CHUNK), F32))
                a_ref[c * N_HEADS + hd] = jnp.where(causal, a, 0.0).astype(BF16)

    s_st = [st_ref[hd] for hd in range(N_HEADS)]
    for c in range(n_chunks):
        r0 = c * CHUNK
        b = b_chunks[c]
        b_last = b[CHUNK - 1:CHUNK, :]
        q_hat = (hq[r0:r0 + CHUNK, :] * jnp.exp(b)).astype(BF16)
        k_hat = (hk[r0:r0 + CHUNK, :] * jnp.exp(b_last - b)).astype(BF16)
        decay = jnp.exp(b_last)
        for hd in range(N_HEADS):
            hs = slice(hd * HGRN_DK, (hd + 1) * HGRN_DK)
            v_h = hv[r0:r0 + CHUNK, hs]
            st = s_st[hd]
            o = _dot(a_ref[c * N_HEADS + hd], v_h) + pl.dot(q_hat[:, hs], st.astype(BF16), trans_b=True)
            s_st[hd] = st * decay[:, hs] + pl.dot(v_h, k_hat[:, hs], trans_a=True)
            yb_ref[r0:r0 + CHUNK, hs] = _rms_rows(o, hng_ref[:, hs]) * hgate[r0:r0 + CHUNK, hs]
    for hd in range(N_HEADS):
        st_ref[hd] = s_st[hd]

    ubuf[U_HALO:U_HALO + tile, :] = proj(C_CA, C_CA + BRANCH_W) * _sigmoid_of_twice(proj(C_CB, C_CB + BRANCH_W))
    gate_w = BRANCH_W
    gates, anchors = [], []
    for j in range((3 * D_MODEL) // gate_w):
        g = jnp.tanh(proj(C_GATE + j * gate_w, C_GATE + (j + 1) * gate_w)) + 1.0
        gates.append(g)
        anchors.append(jnp.where(jnp.maximum(g[0:1, :], g[tile - 1:tile, :]) > 3.0, 1.0, 0.0))
    acc = _causal_depthwise_conv(ubuf, cw_ref, cb_ref[...], tile, U_HALO, anchors)
    ubuf[0:U_HALO, :] = ubuf[tile:tile + U_HALO, :]
    mu = jnp.mean(acc, axis=-1, keepdims=True)
    cen = acc - mu
    var = jnp.mean(cen * cen, axis=-1, keepdims=True)
    yc = _silu_of_twice(cen * lax.rsqrt(var + EPS) * (0.5 * lng_ref[...]) + 0.5 * lnb_ref[...])

    per_branch = D_MODEL // gate_w
    g_a, g_b, g_c = (jnp.concatenate(gates[i * per_branch:(i + 1) * per_branch], axis=1) for i in range(3))
    merged = g_a * _dot(ya_ref[...].astype(BF16), wa_ref[...])
    merged = merged + g_b * _dot(yb_ref[...].astype(BF16), wb_ref[...])
    merged = merged + g_c * _dot(yc.astype(BF16), wc_ref[...])
    o_ref[...] = x + _dot(merged.astype(BF16), wo_ref[...])


def _ffn_kernel(final, x_ref, g_ref, wi_ref, wo_ref, fg_ref, o_ref):
    x = x_ref[...]
    h = _rms_rows(x, g_ref[...]).astype(BF16)
    gate = _dot(h, wi_ref[:, 0:D_FF])
    up = _dot(h, wi_ref[:, D_FF:2 * D_FF])
    y = x + _dot((_silu(gate) * up).astype(BF16), wo_ref[...])
    if final:
        y = _rms_rows(y, fg_ref[...])
    o_ref[...] = y


def _pack_kernel(gate_lo, w_ref, s_ref, o_ref):
    n_small = 2 * N_HEADS
    main = C_GI - gate_lo
    o_ref[:, 0:gate_lo] = (w_ref[:, 0:gate_lo] * s_ref[:, 0:gate_lo]).astype(BF16)
    o_ref[:, gate_lo:C_GI] = (w_ref[:, gate_lo + n_small:gate_lo + n_small + main] * s_ref[:, gate_lo:C_GI]).astype(BF16)
    lane = lax.broadcasted_iota(jnp.int32, (w_ref.shape[0], V7X_LANES), 1) % V7X_SUBLANES
    for blk, first in ((C_GI, gate_lo), (C_GF, gate_lo + N_HEADS)):
        rep = jnp.zeros((w_ref.shape[0], V7X_LANES), F32)
        for hd in range(N_HEADS):
            rep = jnp.where(lane == hd, w_ref[:, first + hd:first + hd + 1], rep)
        o_ref[:, blk:blk + V7X_LANES] = rep.astype(BF16)


def _pack_in_proj(w_in, col_scale, gate_lo):
    depth, d, d_in = w_in.shape
    rows = 256
    return pl.pallas_call(
        functools.partial(_pack_kernel, gate_lo),
        grid=(depth, d // rows),
        in_specs=[pl.BlockSpec((None, rows, d_in), lambda l, r: (l, r, 0)),
                  pl.BlockSpec((1, D_IN_PACKED), lambda l, r: (0, 0))],
        out_specs=pl.BlockSpec((None, rows, D_IN_PACKED), lambda l, r: (l, r, 0)),
        out_shape=jax.ShapeDtypeStruct((depth, d, D_IN_PACKED), BF16),
        compiler_params=pltpu.CompilerParams(
            dimension_semantics=("arbitrary", "arbitrary"),
            vmem_limit_bytes=(V7X_VMEM_BYTES * 3) // 4),
        name="pack_in_proj",
    )(w_in, col_scale[None, :])


def _resident(shape, index):
    return pl.BlockSpec(shape, index, pipeline_mode=pl.Buffered(1))


def _mixer_call(layer, x, ng, w_mix, b_mix, mcw, mcb, mng, lbl, hng, cw, cb, lng, lnb, wa, wb, wc, wo):
    bsz, seq, d = x.shape
    tile = MIX_TILE
    assert seq % tile == 0 and tile % CHUNK == 0 and d == D_MODEL
    depth = lbl.shape[0]

    def lay3(shape):
        return _resident((None,) + shape, lambda b, t: (layer, 0, 0))

    in_specs = [
        pl.BlockSpec((None, tile, d), lambda b, t: (b, t, 0)),
        lay3((1, d)),
        lay3((d, D_IN_PACKED)),
        lay3((1, D_IN_PACKED)),
        lay3((MLSTM_CONV_K, BRANCH_W)),
        lay3((1, BRANCH_W)),
        lay3((1, BRANCH_W)),
        _resident((depth, BRANCH_W), lambda b, t: (0, 0)),
        lay3((1, BRANCH_W)),
        lay3((CONV_K, BRANCH_W)),
        lay3((1, BRANCH_W)),
        lay3((1, BRANCH_W)),
        lay3((1, BRANCH_W)),
        lay3((BRANCH_W, d)),
        lay3((BRANCH_W, d)),
        lay3((BRANCH_W, d)),
        lay3((d, d)),
    ]
    scratch = [
        pltpu.VMEM((tile + QK_HALO, BRANCH_W), F32),
        pltpu.VMEM((tile + U_HALO, BRANCH_W), F32),
        pltpu.VMEM((N_HEADS, MLSTM_DQK, 2 * MLSTM_DV), F32),
        pltpu.VMEM((V7X_SUBLANES, V7X_LANES), F32),
        pltpu.VMEM((N_HEADS, HGRN_DK, HGRN_DK), F32),
        pltpu.VMEM((tile, BRANCH_W), F32),
        pltpu.VMEM((tile, BRANCH_W), F32),
        pltpu.VMEM(((tile // CHUNK) * N_HEADS, CHUNK, CHUNK), BF16),
        pltpu.VMEM((CHUNK, BRANCH_W), F32),
        pltpu.VMEM((CHUNK, BRANCH_W), F32),
        pltpu.VMEM((CHUNK, BRANCH_W), F32),
    ]
    return pl.pallas_call(
        functools.partial(_mixer_kernel, layer, tile),
        grid=(bsz, seq // tile),
        in_specs=in_specs,
        out_specs=pl.BlockSpec((None, tile, d), lambda b, t: (b, t, 0)),
        out_shape=jax.ShapeDtypeStruct(x.shape, x.dtype),
        scratch_shapes=scratch,
        compiler_params=pltpu.CompilerParams(
            dimension_semantics=("arbitrary", "arbitrary"),
            vmem_limit_bytes=(V7X_VMEM_BYTES * 7) // 8),
        name=f"mixer_l{layer}",
    )(x, ng, w_mix, b_mix, mcw, mcb, mng, lbl, hng, cw, cb, lng, lnb, wa, wb, wc, wo)


def _ffn_call(layer, final, x2, g, wi, wo, fg):
    n, d = x2.shape
    tile = FFN_TILE
    assert n % tile == 0

    def lay3(shape):
        return _resident((None,) + shape, lambda i: (layer, 0, 0))

    return pl.pallas_call(
        functools.partial(_ffn_kernel, final),
        grid=(n // tile,),
        in_specs=[
            pl.BlockSpec((tile, d), lambda i: (i, 0)),
            lay3((1, d)),
            lay3((d, 2 * D_FF)),
            lay3((D_FF, d)),
            _resident((1, d), lambda i: (0, 0)),
        ],
        out_specs=pl.BlockSpec((tile, d), lambda i: (i, 0)),
        out_shape=jax.ShapeDtypeStruct(x2.shape, x2.dtype),
        compiler_params=pltpu.CompilerParams(
            dimension_semantics=("arbitrary",),
            vmem_limit_bytes=(V7X_VMEM_BYTES * 3) // 4),
        name=f"ffn_l{layer}",
    )(x2, g, wi, wo, fg)


def kernel(x, norm_mix_g, w_in, b_in, mlstm_conv_w, mlstm_conv_b, mlstm_norm_g, hgrn_lb_logits, hgrn_norm_g,
           conv_w, conv_b, conv_ln_g, conv_ln_b, w_branch_a, w_branch_b, w_branch_c, w_out, norm_ffn_g,
           w_ffn_in, w_ffn_out, final_norm_g):
    depth = w_in.shape[0]
    bsz, seq, d = x.shape
    n_small = 2 * N_HEADS
    gate_lo = 3 * BRANCH_W
    n_grp = V7X_LANES // V7X_SUBLANES

    def replicate(g):
        g = jnp.concatenate([g, jnp.zeros(g.shape[:-1] + (V7X_SUBLANES - N_HEADS,), g.dtype)], axis=-1)
        return jnp.tile(g, (1,) * (g.ndim - 1) + (n_grp,))

    def pack_cols(a):
        return jnp.concatenate(
            [a[..., :gate_lo], a[..., gate_lo + n_small:],
             replicate(a[..., gate_lo:gate_lo + N_HEADS]),
             replicate(a[..., gate_lo + N_HEADS:gate_lo + n_small])], axis=-1)

    halved = jnp.zeros((D_IN_PACKED,), bool)
    for lo, width in ((C_O, BRANCH_W), (C_HG, BRANCH_W), (C_CB, BRANCH_W), (C_GATE, 3 * D_MODEL)):
        halved = halved.at[lo:lo + width].set(True)
    col_scale = jnp.where(halved, 0.5, 1.0).astype(F32)
    w_mix = _pack_in_proj(w_in, col_scale, gate_lo)
    b_mix = (pack_cols(b_in) * col_scale)[:, None, :]
    row = lambda a: a[:, None, :]
    wa = (0.5 * w_branch_a).astype(BF16)
    wb = (0.5 * w_branch_b).astype(BF16)
    wc = (0.5 * w_branch_c).astype(BF16)
    wo = w_out.astype(BF16)
    wfi = w_ffn_in.astype(BF16)
    wfo = w_ffn_out.astype(BF16)
    fg = final_norm_g[None, :]

    for layer in range(depth):
        x = _mixer_call(layer, x, row(norm_mix_g), w_mix, b_mix, mlstm_conv_w, row(mlstm_conv_b),
                        row(mlstm_norm_g), hgrn_lb_logits, row(hgrn_norm_g), conv_w, row(conv_b),
                        row(conv_ln_g), row(conv_ln_b), wa, wb, wc, wo)
        x = _ffn_call(layer, layer == depth - 1, x.reshape(bsz * seq, d), row(norm_ffn_g), wfi, wfo,
                      fg).reshape(bsz, seq, d)
    return x
```

```python
import functools

import jax
import jax.numpy as jnp
from jax import lax
from jax.experimental import pallas as pl
from jax.experimental.pallas import tpu as pltpu

F32 = jnp.float32
BF16 = jnp.bfloat16

D_MODEL = 1024
BRANCH_W = D_MODEL // 2
N_HEADS = 4
MLSTM_DV = BRANCH_W // N_HEADS
MLSTM_DQK = MLSTM_DV // 2
MLSTM_CONV_K = 4
HGRN_DK = BRANCH_W // N_HEADS
CONV_K = 31
D_FF = 2816
EPS = 1e-6
F_FLOOR = 1e-30

V7X_SUBLANES = 8
V7X_LANES = 128
V7X_VMEM_BYTES = 64 * 1024 * 1024

CHUNK = 128
MLSTM_CHUNK = 256
MIX_TILE = 512
FFN_TILE = 512
QK_HALO = V7X_SUBLANES
U_HALO = 4 * V7X_SUBLANES
HGRN_SAFE_DECAY = 150.0

C_QK = 0
C_V = 512
C_O = 1024
C_HF = 1536
C_HQ = 2048
C_HI = 2560
C_HG = 3072
C_CA = 3584
C_CB = 4096
C_GATE = 4608
C_GI = 7680
C_GF = C_GI + V7X_LANES
D_IN_PACKED = C_GF + V7X_LANES


def _sigmoid_pair(z):
    e = jnp.exp(-jnp.abs(z))
    big = 1.0 / (1.0 + e)
    small = e / (1.0 + e)
    pos = z >= 0
    return jnp.where(pos, big, small), jnp.where(pos, small, big)


def _sigmoid_of_twice(zh):
    return 0.5 * jnp.tanh(zh) + 0.5


def _silu_of_twice(zh):
    return zh * (jnp.tanh(zh) + 1.0)


def _log_sigmoid(z):
    return jnp.minimum(z, 0.0) - jnp.log(1.0 + jnp.exp(-jnp.abs(z)))


def _dot(a, b):
    return jnp.dot(a, b, preferred_element_type=F32)


def _split3(a):
    hi = a.astype(BF16)
    r1 = a - hi.astype(F32)
    mid = r1.astype(BF16)
    lo = (r1 - mid.astype(F32)).astype(BF16)
    return hi, mid, lo


def _cumsum_rows(tri, a):
    hi, mid, lo = _split3(a)
    return (_dot(tri, hi) + _dot(tri, mid)) + _dot(tri, lo)


def _rms_rows(y, g):
    return y * lax.rsqrt(jnp.mean(y * y, axis=-1, keepdims=True) + EPS) * g


def _causal_depthwise_conv(buf, w_ref, bias, tile, halo, anchors=()):
    ksize = w_ref.shape[0]
    ext = tile + V7X_SUBLANES
    acc = jnp.broadcast_to(bias, (tile, buf.shape[1]))
    for res in range(min(V7X_SUBLANES, ksize)):
        part = None
        for d in range(res, ksize, V7X_SUBLANES):
            lo = halo - V7X_SUBLANES - (d - res)
            w_row = w_ref[ksize - 1 - d:ksize - d, :]
            if part is None and res < len(anchors):
                w_row = w_row + anchors[res]
            term = w_row * buf[lo:lo + ext, :]
            part = term if part is None else part + term
        acc = acc + part[V7X_SUBLANES - res:V7X_SUBLANES - res + tile, :]
    return acc


def _mixer_kernel(layer, tile,
                  x_ref, ng_ref, w_ref, b_ref, mcw_ref, mcb_ref, mng_ref, lbl_ref, hng_ref,
                  cw_ref, cb_ref, lng_ref, lnb_ref, wa_ref, wb_ref, wc_ref, wo_ref,
                  o_ref,
                  qkbuf, ubuf, ct_ref, m_ref, st_ref, ya_ref, yb_ref, a_ref, hq_ref, hk_ref, hb_ref):
    n_chunks = tile // CHUNK

    @pl.when(pl.program_id(1) == 0)
    def _():
        qkbuf[0:QK_HALO, :] = jnp.zeros((QK_HALO, BRANCH_W), F32)
        ubuf[0:U_HALO, :] = jnp.zeros((U_HALO, BRANCH_W), F32)
        ct_ref[...] = jnp.zeros_like(ct_ref)
        m_ref[...] = jnp.zeros_like(m_ref)
        st_ref[...] = jnp.zeros_like(st_ref)

    x = x_ref[...]
    h = _rms_rows(x, ng_ref[...]).astype(BF16)

    def proj(lo, hi):
        return _dot(h, w_ref[:, lo:hi]) + b_ref[:, lo:hi]

    row_i = lax.broadcasted_iota(jnp.int32, (CHUNK, CHUNK), 0)
    col_i = lax.broadcasted_iota(jnp.int32, (CHUNK, CHUNK), 1)
    causal = col_i <= row_i
    tri = causal.astype(BF16)

    n_mchunks = tile // MLSTM_CHUNK
    mrow_i = lax.broadcasted_iota(jnp.int32, (MLSTM_CHUNK, MLSTM_CHUNK), 0)
    mcol_i = lax.broadcasted_iota(jnp.int32, (MLSTM_CHUNK, MLSTM_CHUNK), 1)
    mcausal = mcol_i <= mrow_i
    mtri = mcausal.astype(BF16)
    qkbuf[QK_HALO:QK_HALO + tile, :] = proj(C_QK, C_QK + BRANCH_W)
    qk = _silu_of_twice(_causal_depthwise_conv(qkbuf, 0.5 * mcw_ref[...], 0.5 * mcb_ref[...], tile, QK_HALO))
    qkbuf[0:QK_HALO, :] = qkbuf[tile:tile + QK_HALO, :]

    zv = proj(C_V, C_V + BRANCH_W)
    og = _sigmoid_of_twice(proj(C_O, C_O + BRANCH_W))
    li_all = proj(C_GI, C_GI + V7X_LANES)
    lf_all = _log_sigmoid(proj(C_GF, C_GF + V7X_LANES))
    lane_c = lax.broadcasted_iota(jnp.int32, (MLSTM_CHUNK, V7X_LANES), 1)
    row_c = lax.broadcasted_iota(jnp.int32, (MLSTM_CHUNK, V7X_LANES), 0)
    lane_grp = lane_c // V7X_SUBLANES
    lane_head = lane_c % V7X_SUBLANES
    ones_blk = jnp.ones((MLSTM_CHUNK, MLSTM_DV), F32)

    def place3(a, first_grp):
        hi, mid, lo = _split3(a)
        zero = jnp.zeros_like(hi)
        return jnp.where(lane_grp == first_grp, hi,
                         jnp.where(lane_grp == first_grp + 1, mid,
                                   jnp.where(lane_grp == first_grp + 2, lo, zero)))

    ones_lo = (lane_grp < 3).astype(BF16)
    ones_hi = jnp.logical_and(lane_grp >= 3, lane_grp < 6).astype(BF16)

    bcs, cs, mus, m_prevs, mu_lasts = [], [], [], [], []
    m_row = m_ref[0:1, :]
    for c in range(n_mchunks):
        r0 = c * MLSTM_CHUNK
        bc = _cumsum_rows(mtri, lf_all[r0:r0 + MLSTM_CHUNK, :])
        cc = li_all[r0:r0 + MLSTM_CHUNK, :] - bc
        cm = cc
        sh = 1
        while sh < MLSTM_CHUNK:
            cm = jnp.where(row_c >= sh, jnp.maximum(cm, pltpu.roll(cm, sh, axis=0)), cm)
            sh *= 2
        mu_last = jnp.maximum(m_row, cm[MLSTM_CHUNK - 1:MLSTM_CHUNK, :])
        bcs.append(bc)
        cs.append(cc)
        mus.append(jnp.maximum(m_row, cm))
        m_prevs.append(m_row)
        mu_lasts.append(mu_last)
        m_row = bc[MLSTM_CHUNK - 1:MLSTM_CHUNK, :] + mu_last
    m_ref[0:1, :] = m_row

    ct_st = [ct_ref[hd] for hd in range(N_HEADS)]
    for c in range(n_mchunks):
        r0 = c * MLSTM_CHUNK
        bc, cc, mu, m_prev, mu_last = bcs[c], cs[c], mus[c], m_prevs[c], mu_lasts[c]
        inter_all = jnp.exp(m_prev - mu)
        rinv_all = jnp.exp(-(bc + mu))
        ws_all = jnp.exp(cc - mu_last)
        keep_all = jnp.exp(m_prev - mu_last)
        lhs = place3(-mu, 0) + ones_hi
        rhs = ones_lo + place3(cc, 3)
        for hd in range(N_HEADS):
            rhs_h = jnp.where(lane_head == hd, rhs, jnp.zeros_like(rhs))
            w = jnp.where(mcausal, jnp.exp(pl.dot(lhs, rhs_h, trans_b=True)), 0.0)
            inter = inter_all[:, hd:hd + 1]

            qh = qk[r0:r0 + MLSTM_CHUNK, hd * MLSTM_DQK:(hd + 1) * MLSTM_DQK] * (MLSTM_DQK ** -0.5)
            kh = qk[r0:r0 + MLSTM_CHUNK, 256 + hd * MLSTM_DQK:256 + (hd + 1) * MLSTM_DQK]
            vh = zv[r0:r0 + MLSTM_CHUNK, hd * MLSTM_DV:(hd + 1) * MLSTM_DV]
            v_ext = jnp.concatenate([vh, ones_blk], axis=1).astype(BF16)
            ct = ct_st[hd]

            sc = pl.dot(qh.astype(BF16), kh.astype(BF16), trans_b=True) * w
            nd = _dot(sc.astype(BF16), v_ext) + _dot((qh * inter).astype(BF16), ct.astype(BF16))
            num = nd[:, 0:MLSTM_DV]
            den = nd[:, MLSTM_DV:2 * MLSTM_DV]
            h_out = num / jnp.maximum(jnp.abs(den), rinv_all[:, hd:hd + 1])
            hn = _rms_rows(h_out, mng_ref[:, hd * MLSTM_DV:(hd + 1) * MLSTM_DV])
            ya_ref[r0:r0 + MLSTM_CHUNK, hd * MLSTM_DV:(hd + 1) * MLSTM_DV] = (
                og[r0:r0 + MLSTM_CHUNK, hd * MLSTM_DV:(hd + 1) * MLSTM_DV] * hn)

            wk = (ws_all[:, hd:hd + 1] * kh).astype(BF16)
            ct_st[hd] = keep_all[:, hd:hd + 1] * ct + pl.dot(wk, v_ext, trans_a=True)

    for hd in range(N_HEADS):
        ct_ref[hd] = ct_st[hd]

    lrows = [lbl_ref[j:j + 1, :] for j in range(lbl_ref.shape[0])]
    lmax = functools.reduce(jnp.maximum, lrows)
    lexp = [jnp.exp(r - lmax) for r in lrows]
    lsum = functools.reduce(jnp.add, lexp)
    lb = jnp.zeros_like(lmax)
    for j in range(1, layer + 1):
        lb = lb + lexp[j] / lsum

    sg_pos, sg_neg = _sigmoid_pair(proj(C_HF, C_HF + BRANCH_W))
    logf = jnp.log(jnp.maximum(lb + (1.0 - lb) * sg_pos, F_FLOOR))
    hk = (1.0 - lb) * sg_neg
    hq = proj(C_HQ, C_HQ + BRANCH_W) * (HGRN_DK ** -0.5)
    hv = proj(C_HI, C_HI + BRANCH_W).astype(BF16)
    hgate = _silu_of_twice(proj(C_HG, C_HG + BRANCH_W))

    b_chunks = [_cumsum_rows(tri, logf[c * CHUNK:(c + 1) * CHUNK, :]) for c in range(n_chunks)]
    worst = functools.reduce(jnp.minimum, [b[CHUNK - 1:CHUNK, :] for b in b_chunks])
    unsafe = jnp.min(worst) < -HGRN_SAFE_DECAY

    @pl.when(jnp.logical_not(unsafe))
    def _():
        for c in range(n_chunks):
            r0 = c * CHUNK
            b = b_chunks[c]
            r = 0.5 * b[CHUNK - 1:CHUNK, :]
            q_t = (hq[r0:r0 + CHUNK, :] * jnp.exp(b - r)).astype(BF16)
            k_t = (hk[r0:r0 + CHUNK, :] * jnp.exp(r - b)).astype(BF16)
            for hd in range(N_HEADS):
                hs = slice(hd * HGRN_DK, (hd + 1) * HGRN_DK)
                a = pl.dot(q_t[:, hs], k_t[:, hs], trans_b=True)
                a_ref[c * N_HEADS + hd] = jnp.where(causal, a, 0.0).astype(BF16)

    @pl.when(unsafe)
    def _():
        for c in range(n_chunks):
            r0 = c * CHUNK
            hq_ref[...] = hq[r0:r0 + CHUNK, :]
            hk_ref[...] = hk[r0:r0 + CHUNK, :]
            hb_ref[...] = b_chunks[c]
            for hd in range(N_HEADS):
                hs = slice(hd * HGRN_DK, (hd + 1) * HGRN_DK)

                def body(grp, a, hs=hs):
                    g0 = pl.multiple_of(grp * V7X_SUBLANES, V7X_SUBLANES)
                    k_g = hk_ref[pl.ds(g0, V7X_SUBLANES), hs]
                    b_g = hb_ref[pl.ds(g0, V7X_SUBLANES), hs]
                    for j in range(V7X_SUBLANES):
                        e = hq_ref[:, hs] * k_g[j:j + 1, :] * jnp.exp(
                            jnp.minimum(hb_ref[:, hs] - b_g[j:j + 1, :], 0.0))
                        a = jnp.where(col_i == g0 + j, jnp.sum(e, axis=-1, keepdims=True), a)
                    return a

                a = lax.fori_loop(0, CHUNK // V7X_SUBLANES, body, jnp.zeros((CHUNK, CHUNK), F32))
                a_ref[c * N_HEADS + hd] = jnp.where(causal, a, 0.0).astype(BF16)

    s_st = [st_ref[hd] for hd in range(N_HEADS)]
    for c in range(n_chunks):
        r0 = c * CHUNK
        b = b_chunks[c]
        b_last = b[CHUNK - 1:CHUNK, :]
        q_hat = (hq[r0:r0 + CHUNK, :] * jnp.exp(b)).astype(BF16)
        k_hat = (hk[r0:r0 + CHUNK, :] * jnp.exp(b_last - b)).astype(BF16)
        decay = jnp.exp(b_last)
        for hd in range(N_HEADS):
            hs = slice(hd * HGRN_DK, (hd + 1) * HGRN_DK)
            v_h = hv[r0:r0 + CHUNK, hs]
            st = s_st[hd]
            o = _dot(a_ref[c * N_HEADS + hd], v_h) + pl.dot(q_hat[:, hs], st.astype(BF16), trans_b=True)
            s_st[hd] = st * decay[:, hs] + pl.dot(v_h, k_hat[:, hs], trans_a=True)
            yb_ref[r0:r0 + CHUNK, hs] = _rms_rows(o, hng_ref[:, hs]) * hgate[r0:r0 + CHUNK, hs]
    for hd in range(N_HEADS):
        st_ref[hd] = s_st[hd]

    ubuf[U_HALO:U_HALO + tile, :] = proj(C_CA, C_CA + BRANCH_W) * _sigmoid_of_twice(proj(C_CB, C_CB + BRANCH_W))
    gate_w = BRANCH_W
    gates, anchors = [], []
    for j in range((3 * D_MODEL) // gate_w):
        g = jnp.tanh(proj(C_GATE + j * gate_w, C_GATE + (j + 1) * gate_w)) + 1.0
        gates.append(g)
        anchors.append(jnp.where(jnp.maximum(g[0:1, :], g[tile - 1:tile, :]) > 3.0, 1.0, 0.0))
    acc = _causal_depthwise_conv(ubuf, cw_ref, cb_ref[...], tile, U_HALO, anchors)
    ubuf[0:U_HALO, :] = ubuf[tile:tile + U_HALO, :]
    mu = jnp.mean(acc, axis=-1, keepdims=True)
    cen = acc - mu
    var = jnp.mean(cen * cen, axis=-1, keepdims=True)
    yc = _silu_of_twice(cen * lax.rsqrt(var + EPS) * (0.5 * lng_ref[...]) + 0.5 * lnb_ref[...])

    per_branch = D_MODEL // gate_w
    g_a, g_b, g_c = (jnp.concatenate(gates[i * per_branch:(i + 1) * per_branch], axis=1) for i in range(3))
    merged = g_a * _dot(ya_ref[...].astype(BF16), wa_ref[...])
    merged = merged + g_b * _dot(yb_ref[...].astype(BF16), wb_ref[...])
    merged = merged + g_c * _dot(yc.astype(BF16), wc_ref[...])
    o_ref[...] = x + _dot(merged.astype(BF16), wo_ref[...])


def _ffn_kernel(final, x_ref, g_ref, wi_ref, wo_ref, fg_ref, o_ref):
    x = x_ref[...]
    h = _rms_rows(x, g_ref[...]).astype(BF16)
    gate = _dot(h, wi_ref[:, 0:D_FF])
    up = _dot(h, wi_ref[:, D_FF:2 * D_FF])
    y = x + _dot((_silu_of_twice(gate) * up).astype(BF16), wo_ref[...])
    if final:
        y = _rms_rows(y, fg_ref[...])
    o_ref[...] = y


def _pack_kernel(gate_lo, w_ref, s_ref, o_ref):
    n_small = 2 * N_HEADS
    main = C_GI - gate_lo
    o_ref[:, 0:gate_lo] = (w_ref[:, 0:gate_lo] * s_ref[:, 0:gate_lo]).astype(BF16)
    o_ref[:, gate_lo:C_GI] = (w_ref[:, gate_lo + n_small:gate_lo + n_small + main] * s_ref[:, gate_lo:C_GI]).astype(BF16)
    lane = lax.broadcasted_iota(jnp.int32, (w_ref.shape[0], V7X_LANES), 1) % V7X_SUBLANES
    for blk, first in ((C_GI, gate_lo), (C_GF, gate_lo + N_HEADS)):
        rep = jnp.zeros((w_ref.shape[0], V7X_LANES), F32)
        for hd in range(N_HEADS):
            rep = jnp.where(lane == hd, w_ref[:, first + hd:first + hd + 1], rep)
        o_ref[:, blk:blk + V7X_LANES] = rep.astype(BF16)


def _pack_in_proj(w_in, col_scale, gate_lo):
    depth, d, d_in = w_in.shape
    rows = 256
    return pl.pallas_call(
        functools.partial(_pack_kernel, gate_lo),
        grid=(depth, d // rows),
        in_specs=[pl.BlockSpec((None, rows, d_in), lambda l, r: (l, r, 0)),
                  pl.BlockSpec((1, D_IN_PACKED), lambda l, r: (0, 0))],
        out_specs=pl.BlockSpec((None, rows, D_IN_PACKED), lambda l, r: (l, r, 0)),
        out_shape=jax.ShapeDtypeStruct((depth, d, D_IN_PACKED), BF16),
        compiler_params=pltpu.CompilerParams(
            dimension_semantics=("arbitrary", "arbitrary"),
            vmem_limit_bytes=(V7X_VMEM_BYTES * 3) // 4),
        name="pack_in_proj",
    )(w_in, col_scale[None, :])


def _resident(shape, index):
    return pl.BlockSpec(shape, index, pipeline_mode=pl.Buffered(1))


def _mixer_call(layer, x, ng, w_mix, b_mix, mcw, mcb, mng, lbl, hng, cw, cb, lng, lnb, wa, wb, wc, wo):
    bsz, seq, d = x.shape
    tile = MIX_TILE
    assert seq % tile == 0 and tile % CHUNK == 0 and d == D_MODEL
    depth = lbl.shape[0]

    def lay3(shape):
        return _resident((None,) + shape, lambda b, t: (layer, 0, 0))

    in_specs = [
        pl.BlockSpec((None, tile, d), lambda b, t: (b, t, 0)),
        lay3((1, d)),
        lay3((d, D_IN_PACKED)),
        lay3((1, D_IN_PACKED)),
        lay3((MLSTM_CONV_K, BRANCH_W)),
        lay3((1, BRANCH_W)),
        lay3((1, BRANCH_W)),
        _resident((depth, BRANCH_W), lambda b, t: (0, 0)),
        lay3((1, BRANCH_W)),
        lay3((CONV_K, BRANCH_W)),
        lay3((1, BRANCH_W)),
        lay3((1, BRANCH_W)),
        lay3((1, BRANCH_W)),
        lay3((BRANCH_W, d)),
        lay3((BRANCH_W, d)),
        lay3((BRANCH_W, d)),
        lay3((d, d)),
    ]
    scratch = [
        pltpu.VMEM((tile + QK_HALO, BRANCH_W), F32),
        pltpu.VMEM((tile + U_HALO, BRANCH_W), F32),
        pltpu.VMEM((N_HEADS, MLSTM_DQK, 2 * MLSTM_DV), F32),
        pltpu.VMEM((V7X_SUBLANES, V7X_LANES), F32),
        pltpu.VMEM((N_HEADS, HGRN_DK, HGRN_DK), F32),
        pltpu.VMEM((tile, BRANCH_W), F32),
        pltpu.VMEM((tile, BRANCH_W), F32),
        pltpu.VMEM(((tile // CHUNK) * N_HEADS, CHUNK, CHUNK), BF16),
        pltpu.VMEM((CHUNK, BRANCH_W), F32),
        pltpu.VMEM((CHUNK, BRANCH_W), F32),
        pltpu.VMEM((CHUNK, BRANCH_W), F32),
    ]
    return pl.pallas_call(
        functools.partial(_mixer_kernel, layer, tile),
        grid=(bsz, seq // tile),
        in_specs=in_specs,
        out_specs=pl.BlockSpec((None, tile, d), lambda b, t: (b, t, 0)),
        out_shape=jax.ShapeDtypeStruct(x.shape, x.dtype),
        scratch_shapes=scratch,
        compiler_params=pltpu.CompilerParams(
            dimension_semantics=("arbitrary", "arbitrary"),
            vmem_limit_bytes=(V7X_VMEM_BYTES * 7) // 8),
        name=f"mixer_l{layer}",
    )(x, ng, w_mix, b_mix, mcw, mcb, mng, lbl, hng, cw, cb, lng, lnb, wa, wb, wc, wo)


def _ffn_call(layer, final, x2, g, wi, wo, fg):
    n, d = x2.shape
    tile = FFN_TILE
    assert n % tile == 0

    def lay3(shape):
        return _resident((None,) + shape, lambda i: (layer, 0, 0))

    return pl.pallas_call(
        functools.partial(_ffn_kernel, final),
        grid=(n // tile,),
        in_specs=[
            pl.BlockSpec((tile, d), lambda i: (i, 0)),
            lay3((1, d)),
            lay3((d, 2 * D_FF)),
            lay3((D_FF, d)),
            _resident((1, d), lambda i: (0, 0)),
        ],
        out_specs=pl.BlockSpec((tile, d), lambda i: (i, 0)),
        out_shape=jax.ShapeDtypeStruct(x2.shape, x2.dtype),
        compiler_params=pltpu.CompilerParams(
            dimension_semantics=("arbitrary",),
            vmem_limit_bytes=(V7X_VMEM_BYTES * 3) // 4),
        name=f"ffn_l{layer}",
    )(x2, g, wi, wo, fg)


def kernel(x, norm_mix_g, w_in, b_in, mlstm_conv_w, mlstm_conv_b, mlstm_norm_g, hgrn_lb_logits, hgrn_norm_g,
           conv_w, conv_b, conv_ln_g, conv_ln_b, w_branch_a, w_branch_b, w_branch_c, w_out, norm_ffn_g,
           w_ffn_in, w_ffn_out, final_norm_g):
    depth = w_in.shape[0]
    bsz, seq, d = x.shape
    n_small = 2 * N_HEADS
    gate_lo = 3 * BRANCH_W
    n_grp = V7X_LANES // V7X_SUBLANES

    def replicate(g):
        g = jnp.concatenate([g, jnp.zeros(g.shape[:-1] + (V7X_SUBLANES - N_HEADS,), g.dtype)], axis=-1)
        return jnp.tile(g, (1,) * (g.ndim - 1) + (n_grp,))

    def pack_cols(a):
        return jnp.concatenate(
            [a[..., :gate_lo], a[..., gate_lo + n_small:],
             replicate(a[..., gate_lo:gate_lo + N_HEADS]),
             replicate(a[..., gate_lo + N_HEADS:gate_lo + n_small])], axis=-1)

    halved = jnp.zeros((D_IN_PACKED,), bool)
    for lo, width in ((C_O, BRANCH_W), (C_HG, BRANCH_W), (C_CB, BRANCH_W), (C_GATE, 3 * D_MODEL)):
        halved = halved.at[lo:lo + width].set(True)
    col_scale = jnp.where(halved, 0.5, 1.0).astype(F32)
    w_mix = _pack_in_proj(w_in, col_scale, gate_lo)
    b_mix = (pack_cols(b_in) * col_scale)[:, None, :]
    row = lambda a: a[:, None, :]
    wa = (0.5 * w_branch_a).astype(BF16)
    wb = (0.5 * w_branch_b).astype(BF16)
    wc = (0.5 * w_branch_c).astype(BF16)
    wo = w_out.astype(BF16)
    ffn_scale = jnp.where(jnp.arange(2 * D_FF) < D_FF, 0.5, 1.0).astype(F32)
    wfi = (w_ffn_in * ffn_scale).astype(BF16)
    wfo = w_ffn_out.astype(BF16)
    fg = final_norm_g[None, :]

    for layer in range(depth):
        x = _mixer_call(layer, x, row(norm_mix_g), w_mix, b_mix, mlstm_conv_w, row(mlstm_conv_b),
                        row(mlstm_norm_g), hgrn_lb_logits, row(hgrn_norm_g), conv_w, row(conv_b),
                        row(conv_ln_g), row(conv_ln_b), wa, wb, wc, wo)
        x = _ffn_call(layer, layer == depth - 1, x.reshape(bsz * seq, d), row(norm_ffn_g), wfi, wfo,
                      fg).reshape(bsz, seq, d)
    return x
```

```python
import functools

import jax
import jax.numpy as jnp
from jax import lax
from jax.experimental import pallas as pl
from jax.experimental.pallas import tpu as pltpu

F32 = jnp.float32
BF16 = jnp.bfloat16

D_MODEL = 1024
BRANCH_W = D_MODEL // 2
N_HEADS = 4
MLSTM_DV = BRANCH_W // N_HEADS
MLSTM_DQK = MLSTM_DV // 2
MLSTM_CONV_K = 4
HGRN_DK = BRANCH_W // N_HEADS
CONV_K = 31
D_FF = 2816
EPS = 1e-6
F_FLOOR = 1e-30

V7X_SUBLANES = 8
V7X_LANES = 128
V7X_VMEM_BYTES = 64 * 1024 * 1024

CHUNK = 128
MLSTM_CHUNK = 256
MIX_TILE = 512
FFN_TILE = 512
QK_HALO = V7X_SUBLANES
U_HALO = 4 * V7X_SUBLANES
HGRN_SAFE_DECAY = 150.0

C_QK = 0
C_V = 512
C_O = 1024
C_HF = 1536
C_HQ = 2048
C_HI = 2560
C_HG = 3072
C_CA = 3584
C_CB = 4096
C_GATE = 4608
C_GI = 7680
C_GF = C_GI + V7X_LANES
D_IN_PACKED = C_GF + V7X_LANES


def _sigmoid_pair(z):
    e = jnp.exp(-jnp.abs(z))
    big = 1.0 / (1.0 + e)
    small = e / (1.0 + e)
    pos = z >= 0
    return jnp.where(pos, big, small), jnp.where(pos, small, big)


def _sigmoid_of_twice(zh):
    return 0.5 * jnp.tanh(zh) + 0.5


def _silu_of_twice(zh):
    return zh * (jnp.tanh(zh) + 1.0)


def _log_sigmoid(z):
    return jnp.minimum(z, 0.0) - jnp.log(1.0 + jnp.exp(-jnp.abs(z)))


def _dot(a, b):
    return jnp.dot(a, b, preferred_element_type=F32)


def _split3(a):
    hi = a.astype(BF16)
    r1 = a - hi.astype(F32)
    mid = r1.astype(BF16)
    lo = (r1 - mid.astype(F32)).astype(BF16)
    return hi, mid, lo


def _cumsum_rows(tri, a):
    hi, mid, lo = _split3(a)
    return (_dot(tri, hi) + _dot(tri, mid)) + _dot(tri, lo)


def _rms_rows(y, g):
    return y * lax.rsqrt(jnp.mean(y * y, axis=-1, keepdims=True) + EPS) * g


def _causal_depthwise_conv(buf, w_ref, bias, tile, halo, anchors=()):
    ksize = w_ref.shape[0]
    ext = tile + V7X_SUBLANES
    acc = jnp.broadcast_to(bias, (tile, buf.shape[1]))
    for res in range(min(V7X_SUBLANES, ksize)):
        part = None
        for d in range(res, ksize, V7X_SUBLANES):
            lo = halo - V7X_SUBLANES - (d - res)
            w_row = w_ref[ksize - 1 - d:ksize - d, :]
            if part is None and res < len(anchors):
                w_row = w_row + anchors[res]
            term = w_row * buf[lo:lo + ext, :]
            part = term if part is None else part + term
        acc = acc + part[V7X_SUBLANES - res:V7X_SUBLANES - res + tile, :]
    return acc


def _mixer_kernel(layer, tile,
                  x_ref, ng_ref, w_ref, b_ref, mcw_ref, mcb_ref, mng_ref, lbl_ref, hng_ref,
                  cw_ref, cb_ref, lng_ref, lnb_ref, wa_ref, wb_ref, wc_ref, wo_ref,
                  o_ref,
                  qkbuf, ubuf, ct_ref, m_ref, st_ref, ya_ref, yb_ref, a_ref, hq_ref, hk_ref, hb_ref):
    n_chunks = tile // CHUNK

    @pl.when(pl.program_id(1) == 0)
    def _():
        qkbuf[0:QK_HALO, :] = jnp.zeros((QK_HALO, BRANCH_W), F32)
        ubuf[0:U_HALO, :] = jnp.zeros((U_HALO, BRANCH_W), F32)
        ct_ref[...] = jnp.zeros_like(ct_ref)
        m_ref[...] = jnp.zeros_like(m_ref)
        st_ref[...] = jnp.zeros_like(st_ref)

    x = x_ref[...]
    h = _rms_rows(x, ng_ref[...]).astype(BF16)

    def proj(lo, hi):
        return _dot(h, w_ref[:, lo:hi]) + b_ref[:, lo:hi]

    row_i = lax.broadcasted_iota(jnp.int32, (CHUNK, CHUNK), 0)
    col_i = lax.broadcasted_iota(jnp.int32, (CHUNK, CHUNK), 1)
    causal = col_i <= row_i
    tri = causal.astype(BF16)

    n_mchunks = tile // MLSTM_CHUNK
    mrow_i = lax.broadcasted_iota(jnp.int32, (MLSTM_CHUNK, MLSTM_CHUNK), 0)
    mcol_i = lax.broadcasted_iota(jnp.int32, (MLSTM_CHUNK, MLSTM_CHUNK), 1)
    mcausal = mcol_i <= mrow_i
    mtri = mcausal.astype(BF16)
    qkbuf[QK_HALO:QK_HALO + tile, :] = proj(C_QK, C_QK + BRANCH_W)
    qk = _silu_of_twice(_causal_depthwise_conv(qkbuf, 0.5 * mcw_ref[...], 0.5 * mcb_ref[...], tile, QK_HALO))
    qkbuf[0:QK_HALO, :] = qkbuf[tile:tile + QK_HALO, :]

    zv = proj(C_V, C_V + BRANCH_W)
    og = _sigmoid_of_twice(proj(C_O, C_O + BRANCH_W))
    li_all = proj(C_GI, C_GI + V7X_LANES)
    lf_all = _log_sigmoid(proj(C_GF, C_GF + V7X_LANES))
    lane_c = lax.broadcasted_iota(jnp.int32, (MLSTM_CHUNK, V7X_LANES), 1)
    row_c = lax.broadcasted_iota(jnp.int32, (MLSTM_CHUNK, V7X_LANES), 0)
    lane_grp = lane_c // V7X_SUBLANES
    lane_head = lane_c % V7X_SUBLANES
    ones_blk = jnp.ones((MLSTM_CHUNK, MLSTM_DV), F32)

    def place3(a, first_grp):
        hi, mid, lo = _split3(a)
        zero = jnp.zeros_like(hi)
        return jnp.where(lane_grp == first_grp, hi,
                         jnp.where(lane_grp == first_grp + 1, mid,
                                   jnp.where(lane_grp == first_grp + 2, lo, zero)))

    ones_lo = (lane_grp < 3).astype(BF16)
    ones_hi = jnp.logical_and(lane_grp >= 3, lane_grp < 6).astype(BF16)

    bcs, cs, mus, m_prevs, mu_lasts = [], [], [], [], []
    m_row = m_ref[0:1, :]
    for c in range(n_mchunks):
        r0 = c * MLSTM_CHUNK
        bc = _cumsum_rows(mtri, lf_all[r0:r0 + MLSTM_CHUNK, :])
        cc = li_all[r0:r0 + MLSTM_CHUNK, :] - bc
        cm = cc
        sh = 1
        while sh < MLSTM_CHUNK:
            cm = jnp.where(row_c >= sh, jnp.maximum(cm, pltpu.roll(cm, sh, axis=0)), cm)
            sh *= 2
        mu_last = jnp.maximum(m_row, cm[MLSTM_CHUNK - 1:MLSTM_CHUNK, :])
        bcs.append(bc)
        cs.append(cc)
        mus.append(jnp.maximum(m_row, cm))
        m_prevs.append(m_row)
        mu_lasts.append(mu_last)
        m_row = bc[MLSTM_CHUNK - 1:MLSTM_CHUNK, :] + mu_last
    m_ref[0:1, :] = m_row

    ct_st = [ct_ref[hd] for hd in range(N_HEADS)]
    for c in range(n_mchunks):
        r0 = c * MLSTM_CHUNK
        bc, cc, mu, m_prev, mu_last = bcs[c], cs[c], mus[c], m_prevs[c], mu_lasts[c]
        inter_all = jnp.exp(m_prev - mu)
        rinv_all = jnp.exp(-(bc + mu))
        ws_all = jnp.exp(cc - mu_last)
        keep_all = jnp.exp(m_prev - mu_last)
        lhs = place3(-mu, 0) + ones_hi
        rhs = ones_lo + place3(cc, 3)
        for hd in range(N_HEADS):
            rhs_h = jnp.where(lane_head == hd, rhs, jnp.zeros_like(rhs))
            w = jnp.where(mcausal, jnp.exp(pl.dot(lhs, rhs_h, trans_b=True)), 0.0)
            inter = inter_all[:, hd:hd + 1]

            qh = qk[r0:r0 + MLSTM_CHUNK, hd * MLSTM_DQK:(hd + 1) * MLSTM_DQK] * (MLSTM_DQK ** -0.5)
            kh = qk[r0:r0 + MLSTM_CHUNK, 256 + hd * MLSTM_DQK:256 + (hd + 1) * MLSTM_DQK]
            vh = zv[r0:r0 + MLSTM_CHUNK, hd * MLSTM_DV:(hd + 1) * MLSTM_DV]
            v_ext = jnp.concatenate([vh, ones_blk], axis=1).astype(BF16)
            ct = ct_st[hd]

            sc = pl.dot(qh.astype(BF16), kh.astype(BF16), trans_b=True) * w
            nd = _dot(sc.astype(BF16), v_ext) + _dot((qh * inter).astype(BF16), ct.astype(BF16))
            num = nd[:, 0:MLSTM_DV]
            den = nd[:, MLSTM_DV:2 * MLSTM_DV]
            h_out = num / jnp.maximum(jnp.abs(den), rinv_all[:, hd:hd + 1])
            hn = _rms_rows(h_out, mng_ref[:, hd * MLSTM_DV:(hd + 1) * MLSTM_DV])
            ya_ref[r0:r0 + MLSTM_CHUNK, hd * MLSTM_DV:(hd + 1) * MLSTM_DV] = (
                og[r0:r0 + MLSTM_CHUNK, hd * MLSTM_DV:(hd + 1) * MLSTM_DV] * hn)

            wk = (ws_all[:, hd:hd + 1] * kh).astype(BF16)
            ct_st[hd] = keep_all[:, hd:hd + 1] * ct + pl.dot(wk, v_ext, trans_a=True)

    for hd in range(N_HEADS):
        ct_ref[hd] = ct_st[hd]

    lrows = [lbl_ref[j:j + 1, :] for j in range(lbl_ref.shape[0])]
    lmax = functools.reduce(jnp.maximum, lrows)
    lexp = [jnp.exp(r - lmax) for r in lrows]
    lsum = functools.reduce(jnp.add, lexp)
    lb = jnp.zeros_like(lmax)
    for j in range(1, layer + 1):
        lb = lb + lexp[j] / lsum

    sg_pos, sg_neg = _sigmoid_pair(proj(C_HF, C_HF + BRANCH_W))
    logf = jnp.log(jnp.maximum(lb + (1.0 - lb) * sg_pos, F_FLOOR))
    hk = (1.0 - lb) * sg_neg
    hq = proj(C_HQ, C_HQ + BRANCH_W) * (HGRN_DK ** -0.5)
    hv = proj(C_HI, C_HI + BRANCH_W).astype(BF16)
    hgate = _silu_of_twice(proj(C_HG, C_HG + BRANCH_W))

    b_chunks = [_cumsum_rows(tri, logf[c * CHUNK:(c + 1) * CHUNK, :]) for c in range(n_chunks)]
    worst = functools.reduce(jnp.minimum, [b[CHUNK - 1:CHUNK, :] for b in b_chunks])
    unsafe = jnp.min(worst) < -HGRN_SAFE_DECAY

    @pl.when(jnp.logical_not(unsafe))
    def _():
        for c in range(n_chunks):
            r0 = c * CHUNK
            b = b_chunks[c]
            r = 0.5 * b[CHUNK - 1:CHUNK, :]
            q_t = (hq[r0:r0 + CHUNK, :] * jnp.exp(b - r)).astype(BF16)
            k_t = (hk[r0:r0 + CHUNK, :] * jnp.exp(r - b)).astype(BF16)
            for hd in range(N_HEADS):
                hs = slice(hd * HGRN_DK, (hd + 1) * HGRN_DK)
                a = pl.dot(q_t[:, hs], k_t[:, hs], trans_b=True)
                a_ref[c * N_HEADS + hd] = jnp.where(causal, a, 0.0).astype(BF16)

    @pl.when(unsafe)
    def _():
        for c in range(n_chunks):
            r0 = c * CHUNK
            hq_ref[...] = hq[r0:r0 + CHUNK, :]
            hk_ref[...] = hk[r0:r0 + CHUNK, :]
            hb_ref[...] = b_chunks[c]
            for hd in range(N_HEADS):
                hs = slice(hd * HGRN_DK, (hd + 1) * HGRN_DK)

                def body(grp, a, hs=hs):
                    g0 = pl.multiple_of(grp * V7X_SUBLANES, V7X_SUBLANES)
                    k_g = hk_ref[pl.ds(g0, V7X_SUBLANES), hs]
                    b_g = hb_ref[pl.ds(g0, V7X_SUBLANES), hs]
                    for j in range(V7X_SUBLANES):
                        e = hq_ref[:, hs] * k_g[j:j + 1, :] * jnp.exp(
                            jnp.minimum(hb_ref[:, hs] - b_g[j:j + 1, :], 0.0))
                        a = jnp.where(col_i == g0 + j, jnp.sum(e, axis=-1, keepdims=True), a)
                    return a

                a = lax.fori_loop(0, CHUNK // V7X_SUBLANES, body, jnp.zeros((CHUNK, CHUNK), F32))
                a_ref[c * N_HEADS + hd] = jnp.where(causal, a, 0.0).astype(BF16)

    s_st = [st_ref[hd] for hd in range(N_HEADS)]
    for c in range(n_chunks):
        r0 = c * CHUNK
        b = b_chunks[c]
        b_last = b[CHUNK - 1:CHUNK, :]
        q_hat = (hq[r0:r0 + CHUNK, :] * jnp.exp(b)).astype(BF16)
        k_hat = (hk[r0:r0 + CHUNK, :] * jnp.exp(b_last - b)).astype(BF16)
        decay = jnp.exp(b_last)
        for hd in range(N_HEADS):
            hs = slice(hd * HGRN_DK, (hd + 1) * HGRN_DK)
            v_h = hv[r0:r0 + CHUNK, hs]
            st = s_st[hd]
            o = _dot(a_ref[c * N_HEADS + hd], v_h) + pl.dot(q_hat[:, hs], st.astype(BF16), trans_b=True)
            s_st[hd] = st * decay[:, hs] + pl.dot(v_h, k_hat[:, hs], trans_a=True)
            yb_ref[r0:r0 + CHUNK, hs] = _rms_rows(o, hng_ref[:, hs]) * hgate[r0:r0 + CHUNK, hs]
    for hd in range(N_HEADS):
        st_ref[hd] = s_st[hd]

    ubuf[U_HALO:U_HALO + tile, :] = proj(C_CA, C_CA + BRANCH_W) * _sigmoid_of_twice(proj(C_CB, C_CB + BRANCH_W))
    gate_w = BRANCH_W // 2
    gates, anchors = [], []
    for j in range((3 * D_MODEL) // gate_w):
        g = jnp.tanh(proj(C_GATE + j * gate_w, C_GATE + (j + 1) * gate_w)) + 1.0
        gates.append(g)
        zero = jnp.where(jnp.maximum(g[0:1, :], g[tile - 1:tile, :]) > 3.0, 1.0, 0.0)
        anchors.append(jnp.concatenate([zero] * (BRANCH_W // gate_w), axis=1))
    acc = _causal_depthwise_conv(ubuf, cw_ref, cb_ref[...], tile, U_HALO, anchors)
    ubuf[0:U_HALO, :] = ubuf[tile:tile + U_HALO, :]
    mu = jnp.mean(acc, axis=-1, keepdims=True)
    cen = acc - mu
    var = jnp.mean(cen * cen, axis=-1, keepdims=True)
    yc = _silu_of_twice(cen * lax.rsqrt(var + EPS) * (0.5 * lng_ref[...]) + 0.5 * lnb_ref[...])

    per_branch = D_MODEL // gate_w
    g_a, g_b, g_c = (jnp.concatenate(gates[i * per_branch:(i + 1) * per_branch], axis=1) for i in range(3))
    merged = g_a * _dot(ya_ref[...].astype(BF16), wa_ref[...])
    merged = merged + g_b * _dot(yb_ref[...].astype(BF16), wb_ref[...])
    merged = merged + g_c * _dot(yc.astype(BF16), wc_ref[...])
    o_ref[...] = x + _dot(merged.astype(BF16), wo_ref[...])


def _ffn_kernel(final, x_ref, g_ref, wi_ref, wo_ref, fg_ref, o_ref):
    x = x_ref[...]
    h = _rms_rows(x, g_ref[...]).astype(BF16)
    gate = _dot(h, wi_ref[:, 0:D_FF])
    up = _dot(h, wi_ref[:, D_FF:2 * D_FF])
    y = x + _dot((_silu_of_twice(gate) * up).astype(BF16), wo_ref[...])
    if final:
        y = _rms_rows(y, fg_ref[...])
    o_ref[...] = y


def _pack_kernel(gate_lo, w_ref, s_ref, o_ref):
    n_small = 2 * N_HEADS
    main = C_GI - gate_lo
    o_ref[:, 0:gate_lo] = (w_ref[:, 0:gate_lo] * s_ref[:, 0:gate_lo]).astype(BF16)
    o_ref[:, gate_lo:C_GI] = (w_ref[:, gate_lo + n_small:gate_lo + n_small + main] * s_ref[:, gate_lo:C_GI]).astype(BF16)
    lane = lax.broadcasted_iota(jnp.int32, (w_ref.shape[0], V7X_LANES), 1) % V7X_SUBLANES
    for blk, first in ((C_GI, gate_lo), (C_GF, gate_lo + N_HEADS)):
        rep = jnp.zeros((w_ref.shape[0], V7X_LANES), F32)
        for hd in range(N_HEADS):
            rep = jnp.where(lane == hd, w_ref[:, first + hd:first + hd + 1], rep)
        o_ref[:, blk:blk + V7X_LANES] = rep.astype(BF16)


def _pack_in_proj(w_in, col_scale, gate_lo):
    depth, d, d_in = w_in.shape
    rows = 256
    return pl.pallas_call(
        functools.partial(_pack_kernel, gate_lo),
        grid=(depth, d // rows),
        in_specs=[pl.BlockSpec((None, rows, d_in), lambda l, r: (l, r, 0)),
                  pl.BlockSpec((1, D_IN_PACKED), lambda l, r: (0, 0))],
        out_specs=pl.BlockSpec((None, rows, D_IN_PACKED), lambda l, r: (l, r, 0)),
        out_shape=jax.ShapeDtypeStruct((depth, d, D_IN_PACKED), BF16),
        compiler_params=pltpu.CompilerParams(
            dimension_semantics=("arbitrary", "arbitrary"),
            vmem_limit_bytes=(V7X_VMEM_BYTES * 3) // 4),
        name="pack_in_proj",
    )(w_in, col_scale[None, :])


def _resident(shape, index):
    return pl.BlockSpec(shape, index, pipeline_mode=pl.Buffered(1))


def _mixer_call(layer, x, ng, w_mix, b_mix, mcw, mcb, mng, lbl, hng, cw, cb, lng, lnb, wa, wb, wc, wo):
    bsz, seq, d = x.shape
    tile = MIX_TILE
    assert seq % tile == 0 and tile % CHUNK == 0 and d == D_MODEL
    depth = lbl.shape[0]

    def lay3(shape):
        return _resident((None,) + shape, lambda b, t: (layer, 0, 0))

    in_specs = [
        pl.BlockSpec((None, tile, d), lambda b, t: (b, t, 0)),
        lay3((1, d)),
        lay3((d, D_IN_PACKED)),
        lay3((1, D_IN_PACKED)),
        lay3((MLSTM_CONV_K, BRANCH_W)),
        lay3((1, BRANCH_W)),
        lay3((1, BRANCH_W)),
        _resident((depth, BRANCH_W), lambda b, t: (0, 0)),
        lay3((1, BRANCH_W)),
        lay3((CONV_K, BRANCH_W)),
        lay3((1, BRANCH_W)),
        lay3((1, BRANCH_W)),
        lay3((1, BRANCH_W)),
        lay3((BRANCH_W, d)),
        lay3((BRANCH_W, d)),
        lay3((BRANCH_W, d)),
        lay3((d, d)),
    ]
    scratch = [
        pltpu.VMEM((tile + QK_HALO, BRANCH_W), F32),
        pltpu.VMEM((tile + U_HALO, BRANCH_W), F32),
        pltpu.VMEM((N_HEADS, MLSTM_DQK, 2 * MLSTM_DV), F32),
        pltpu.VMEM((V7X_SUBLANES, V7X_LANES), F32),
        pltpu.VMEM((N_HEADS, HGRN_DK, HGRN_DK), F32),
        pltpu.VMEM((tile, BRANCH_W), F32),
        pltpu.VMEM((tile, BRANCH_W), F32),
        pltpu.VMEM(((tile // CHUNK) * N_HEADS, CHUNK, CHUNK), BF16),
        pltpu.VMEM((CHUNK, BRANCH_W), F32),
        pltpu.VMEM((CHUNK, BRANCH_W), F32),
        pltpu.VMEM((CHUNK, BRANCH_W), F32),
    ]
    return pl.pallas_call(
        functools.partial(_mixer_kernel, layer, tile),
        grid=(bsz, seq // tile),
        in_specs=in_specs,
        out_specs=pl.BlockSpec((None, tile, d), lambda b, t: (b, t, 0)),
        out_shape=jax.ShapeDtypeStruct(x.shape, x.dtype),
        scratch_shapes=scratch,
        compiler_params=pltpu.CompilerParams(
            dimension_semantics=("arbitrary", "arbitrary"),
            vmem_limit_bytes=(V7X_VMEM_BYTES * 7) // 8),
        name=f"mixer_l{layer}",
    )(x, ng, w_mix, b_mix, mcw, mcb, mng, lbl, hng, cw, cb, lng, lnb, wa, wb, wc, wo)


def _ffn_call(layer, final, x2, g, wi, wo, fg):
    n, d = x2.shape
    tile = FFN_TILE
    assert n % tile == 0

    def lay3(shape):
        return _resident((None,) + shape, lambda i: (layer, 0, 0))

    return pl.pallas_call(
        functools.partial(_ffn_kernel, final),
        grid=(n // tile,),
        in_specs=[
            pl.BlockSpec((tile, d), lambda i: (i, 0)),
            lay3((1, d)),
            lay3((d, 2 * D_FF)),
            lay3((D_FF, d)),
            _resident((1, d), lambda i: (0, 0)),
        ],
        out_specs=pl.BlockSpec((tile, d), lambda i: (i, 0)),
        out_shape=jax.ShapeDtypeStruct(x2.shape, x2.dtype),
        compiler_params=pltpu.CompilerParams(
            dimension_semantics=("arbitrary",),
            vmem_limit_bytes=(V7X_VMEM_BYTES * 3) // 4),
        name=f"ffn_l{layer}",
    )(x2, g, wi, wo, fg)


def kernel(x, norm_mix_g, w_in, b_in, mlstm_conv_w, mlstm_conv_b, mlstm_norm_g, hgrn_lb_logits, hgrn_norm_g,
           conv_w, conv_b, conv_ln_g, conv_ln_b, w_branch_a, w_branch_b, w_branch_c, w_out, norm_ffn_g,
           w_ffn_in, w_ffn_out, final_norm_g):
    depth = w_in.shape[0]
    bsz, seq, d = x.shape
    n_small = 2 * N_HEADS
    gate_lo = 3 * BRANCH_W
    n_grp = V7X_LANES // V7X_SUBLANES

    def replicate(g):
        g = jnp.concatenate([g, jnp.zeros(g.shape[:-1] + (V7X_SUBLANES - N_HEADS,), g.dtype)], axis=-1)
        return jnp.tile(g, (1,) * (g.ndim - 1) + (n_grp,))

    def pack_cols(a):
        return jnp.concatenate(
            [a[..., :gate_lo], a[..., gate_lo + n_small:],
             replicate(a[..., gate_lo:gate_lo + N_HEADS]),
             replicate(a[..., gate_lo + N_HEADS:gate_lo + n_small])], axis=-1)

    halved = jnp.zeros((D_IN_PACKED,), bool)
    for lo, width in ((C_O, BRANCH_W), (C_HG, BRANCH_W), (C_CB, BRANCH_W), (C_GATE, 3 * D_MODEL)):
        halved = halved.at[lo:lo + width].set(True)
    col_scale = jnp.where(halved, 0.5, 1.0).astype(F32)
    w_mix = _pack_in_proj(w_in, col_scale, gate_lo)
    b_mix = (pack_cols(b_in) * col_scale)[:, None, :]
    row = lambda a: a[:, None, :]
    wa = (0.5 * w_branch_a).astype(BF16)
    wb = (0.5 * w_branch_b).astype(BF16)
    wc = (0.5 * w_branch_c).astype(BF16)
    wo = w_out.astype(BF16)
    ffn_scale = jnp.where(jnp.arange(2 * D_FF) < D_FF, 0.5, 1.0).astype(F32)
    wfi = (w_ffn_in * ffn_scale).astype(BF16)
    wfo = w_ffn_out.astype(BF16)
    fg = final_norm_g[None, :]

    for layer in range(depth):
        x = _mixer_call(layer, x, row(norm_mix_g), w_mix, b_mix, mlstm_conv_w, row(mlstm_conv_b),
                        row(mlstm_norm_g), hgrn_lb_logits, row(hgrn_norm_g), conv_w, row(conv_b),
                        row(conv_ln_g), row(conv_ln_b), wa, wb, wc, wo)
        x = _ffn_call(layer, layer == depth - 1, x.reshape(bsz * seq, d), row(norm_ffn_g), wfi, wfo,
                      fg).reshape(bsz, seq, d)
    return x
```
